```python
import math
import jax, jax.numpy as jnp
from jax import lax
import numpy as np

D_MODEL = 1024
BATCH = 4
SEQ = 8192
DEPTH = 2

N_MIXERS = 2
D_FF = 2816
RMS_EPS = 1e-6
LN_EPS = 1e-5
DN_HEAD_DIM = 128
DN_HEADS = D_MODEL // DN_HEAD_DIM
DN_WIDTH = DN_HEADS * DN_HEAD_DIM
DN_CONV = 4
DN_CHUNK = 64
SG_WIDTH = 2 * D_MODEL
SG_GROUPS = 8
SG_CHUNK = 128
N_A = (DEPTH + 1) // 2
N_B = DEPTH // 2

kernel_name = 'hybrid_deltanet_spatialgate_macaron'


def rmsnorm(x, g, eps=RMS_EPS):
    xf = x.astype(jnp.float32)
    y = xf * lax.rsqrt(jnp.mean(xf * xf, axis=-1, keepdims=True) + eps)
    return (y * g.astype(jnp.float32)).astype(x.dtype)


def layernorm(x, g, b, eps=LN_EPS):
    xf = x.astype(jnp.float32)
    mu = jnp.mean(xf, axis=-1, keepdims=True)
    xc = xf - mu
    y = xc * lax.rsqrt(jnp.mean(xc * xc, axis=-1, keepdims=True) + eps)
    return (y * g.astype(jnp.float32) + b.astype(jnp.float32)).astype(x.dtype)


def l2norm(x, eps=1e-6):
    return x * lax.rsqrt(jnp.sum(x * x, axis=-1, keepdims=True) + eps)


def swiglu(h, w_gate, w_up, w_down):
    return (jax.nn.silu(h @ w_gate) * (h @ w_up)) @ w_down


def causal_short_conv(x, w):
    K = w.shape[0]
    S = x.shape[1]
    xp = jnp.pad(x, ((0, 0), (K - 1, 0), (0, 0)))
    return sum(xp[:, j:j + S, :] * w[j] for j in range(K))


def gated_delta_rule(q, k, v, g, beta):
    B, H, S, Dk = q.shape
    Dv = v.shape[-1]
    C = DN_CHUNK
    N = S // C
    q = q * (Dk ** -0.5)
    q = q.reshape(B, H, N, C, Dk)
    k = k.reshape(B, H, N, C, Dk)
    v = v.reshape(B, H, N, C, Dv)
    g = g.reshape(B, H, N, C)
    beta = beta.reshape(B, H, N, C)
    gc = jnp.cumsum(g, axis=-1)
    causal = jnp.tril(jnp.ones((C, C), dtype=bool))
    strict = jnp.tril(jnp.ones((C, C), dtype=bool), -1)
    diff = gc[..., :, None] - gc[..., None, :]
    decay = jnp.where(causal, jnp.exp(jnp.where(causal, diff, 0.0)), 0.0)
    k_beta = k * beta[..., None]
    v_beta = v * beta[..., None]
    L = jnp.where(strict, jnp.einsum('bhnid,bhnjd->bhnij', k_beta, k) * decay, 0.0)
    A = L + jnp.eye(C, dtype=jnp.float32)
    rhs = jnp.concatenate([v_beta, k_beta * jnp.exp(gc)[..., None]], axis=-1)
    sol = lax.linalg.triangular_solve(A, rhs, left_side=True, lower=True, unit_diagonal=True)
    u = sol[..., :Dv]
    w = sol[..., Dv:]
    attn = jnp.where(causal, jnp.einsum('bhnid,bhnjd->bhnij', q, k) * decay, 0.0)
    q_dec = q * jnp.exp(gc)[..., None]
    k_dec = k * jnp.exp(gc[..., -1:] - gc)[..., None]
    g_last = jnp.exp(gc[..., -1])

    def step(state, xs):
        u_i, w_i, attn_i, qd_i, kd_i, gl_i = xs
        v_new = u_i - jnp.einsum('bhck,bhkv->bhcv', w_i, state)
        o = jnp.einsum('bhck,bhkv->bhcv', qd_i, state) + jnp.einsum('bhij,bhjv->bhiv', attn_i, v_new)
        state = state * gl_i[..., None, None] + jnp.einsum('bhck,bhcv->bhkv', kd_i, v_new)
        return state, o

    xs = tuple(jnp.moveaxis(t, 2, 0) for t in (u, w, attn, q_dec, k_dec, g_last))
    s0 = jnp.zeros((B, H, Dk, Dv), jnp.float32)
    _, o = lax.scan(step, s0, xs)
    return jnp.moveaxis(o, 0, 2).reshape(B, H, S, Dv)


def gated_deltanet(h, w_in, conv_w, a_log, dt_bias, norm_g, w_out):
    B, S, _ = h.shape
    H, Dh, W = DN_HEADS, DN_HEAD_DIM, DN_WIDTH
    f32 = jnp.float32
    proj = h @ w_in
    qkv = proj[..., :3 * W]
    z = proj[..., 3 * W:4 * W]
    b_raw = proj[..., 4 * W:4 * W + H]
    a_raw = proj[..., 4 * W + H:]
    qkv = jax.nn.silu(causal_short_conv(qkv, conv_w))
    q, k, v = jnp.split(qkv, 3, axis=-1)
    to_heads = lambda t: t.reshape(B, S, H, Dh).transpose(0, 2, 1, 3).astype(f32)
    q = l2norm(to_heads(q))
    k = l2norm(to_heads(k))
    v = to_heads(v)
    beta = jax.nn.sigmoid(b_raw.astype(f32)).transpose(0, 2, 1)
    g = (-jnp.exp(a_log.astype(f32)) *
         jax.nn.softplus(a_raw.astype(f32) + dt_bias.astype(f32))).transpose(0, 2, 1)
    o = gated_delta_rule(q, k, v, g, beta).transpose(0, 2, 1, 3)
    o = rmsnorm(o, norm_g) * jax.nn.silu(z.reshape(B, S, H, Dh).astype(f32))
    return o.reshape(B, S, W).astype(h.dtype) @ w_out


def spatial_gating(h, w_in, b_in, ln_g, ln_b, w_s, b_s, w_out):
    B, S, _ = h.shape
    E, G, C = SG_WIDTH, SG_GROUPS, SG_CHUNK
    N = S // C
    zz = jax.nn.gelu(h @ w_in + b_in, approximate=False)
    u = zz[..., :E]
    v = layernorm(zz[..., E:], ln_g, ln_b)
    mask = jnp.tril(jnp.ones((C, C), dtype=bool))
    w_c = jnp.where(mask, w_s, 0.0)
    vg = v.reshape(B, N, C, G, E // G)
    mixed = jnp.einsum('gts,bnsgc->bntgc', w_c, vg) + b_s.T[None, None, :, :, None]
    return (u * mixed.reshape(B, S, E)) @ w_out


def setup_inputs(seed: int = 0) -> dict:
    key = jax.random.key(seed)
    ks = jax.random.split(key, 20)
    D, F, H, Dh, W = D_MODEL, D_FF, DN_HEADS, DN_HEAD_DIM, DN_WIDTH
    E, G, C = SG_WIDTH, SG_GROUPS, SG_CHUNK
    nrm = jax.random.normal
    x = nrm(ks[0], (BATCH, SEQ, D), jnp.float32)
    norm_g = 1.0 + 0.02 * nrm(ks[1], (DEPTH, 6, D), jnp.float32)
    ffn_w_gate = nrm(ks[2], (DEPTH, 2, D, F), jnp.float32) * D ** -0.5
    ffn_w_up = nrm(ks[3], (DEPTH, 2, D, F), jnp.float32) * D ** -0.5
    ffn_w_down = nrm(ks[4], (DEPTH, 2, F, D), jnp.float32) * F ** -0.5
    dn_w_in = nrm(ks[5], (N_A, D, 4 * W + 2 * H), jnp.float32) * D ** -0.5
    dn_conv_w = nrm(ks[6], (N_A, DN_CONV, 3 * W), jnp.float32) * DN_CONV ** -0.5
    dn_a_log = jnp.log(jax.random.uniform(ks[7], (N_A, H), jnp.float32, minval=1.0, maxval=16.0))
    dt = jnp.exp(jax.random.uniform(ks[8], (N_A, H), jnp.float32,
                                    minval=math.log(1e-3), maxval=math.log(1e-1)))
    dn_dt_bias = dt + jnp.log(-jnp.expm1(-dt))
    dn_norm_g = 1.0 + 0.02 * nrm(ks[9], (N_A, Dh), jnp.float32)
    dn_w_out = nrm(ks[10], (N_A, W, D), jnp.float32) * W ** -0.5
    sg_w_in = nrm(ks[11], (N_B, D, 2 * E), jnp.float32) * D ** -0.5
    sg_b_in = 0.02 * nrm(ks[12], (N_B, 2 * E), jnp.float32)
    sg_ln_g = 1.0 + 0.02 * nrm(ks[13], (N_B, E), jnp.float32)
    sg_ln_b = 0.02 * nrm(ks[14], (N_B, E), jnp.float32)
    sg_w_s = nrm(ks[15], (N_B, G, C, C), jnp.float32) * C ** -0.5
    sg_b_s = 1.0 + 0.02 * nrm(ks[16], (N_B, G, C), jnp.float32)
    sg_w_out = nrm(ks[17], (N_B, E, D), jnp.float32) * E ** -0.5
    return {'x': x, 'norm_g': norm_g, 'ffn_w_gate': ffn_w_gate, 'ffn_w_up': ffn_w_up,
            'ffn_w_down': ffn_w_down, 'dn_w_in': dn_w_in, 'dn_conv_w': dn_conv_w,
            'dn_a_log': dn_a_log, 'dn_dt_bias': dn_dt_bias, 'dn_norm_g': dn_norm_g,
            'dn_w_out': dn_w_out, 'sg_w_in': sg_w_in, 'sg_b_in': sg_b_in, 'sg_ln_g': sg_ln_g,
            'sg_ln_b': sg_ln_b, 'sg_w_s': sg_w_s, 'sg_b_s': sg_b_s, 'sg_w_out': sg_w_out}


def reference(x, norm_g, ffn_w_gate, ffn_w_up, ffn_w_down, dn_w_in, dn_conv_w, dn_a_log,
              dn_dt_bias, dn_norm_g, dn_w_out, sg_w_in, sg_b_in, sg_ln_g, sg_ln_b, sg_w_s,
              sg_b_s, sg_w_out):
    for i in range(DEPTH):
        ng = norm_g[i]
        h = rmsnorm(x, ng[0])
        x = x + 0.5 * rmsnorm(swiglu(h, ffn_w_gate[i, 0], ffn_w_up[i, 0], ffn_w_down[i, 0]), ng[1])
        h = rmsnorm(x, ng[2])
        j = i // N_MIXERS
        if i % N_MIXERS == 0:
            m = gated_deltanet(h, dn_w_in[j], dn_conv_w[j], dn_a_log[j], dn_dt_bias[j],
                               dn_norm_g[j], dn_w_out[j])
        else:
            m = spatial_gating(h, sg_w_in[j], sg_b_in[j], sg_ln_g[j], sg_ln_b[j], sg_w_s[j],
                               sg_b_s[j], sg_w_out[j])
        x = x + rmsnorm(m, ng[3])
        h = rmsnorm(x, ng[4])
        x = x + 0.5 * rmsnorm(swiglu(h, ffn_w_gate[i, 1], ffn_w_up[i, 1], ffn_w_down[i, 1]), ng[5])
    return x
```

```python
import functools
import math

import jax
import jax.numpy as jnp
from jax import lax
from jax.experimental import pallas as pl
from jax.experimental.pallas import tpu as pltpu

RMS_EPS = 1e-6
LN_EPS = 1e-5
L2_EPS = 1e-6
DN_HEADS = 8
DN_HEAD_DIM = 128
DN_CONV = 4
DN_CHUNK = 64
SG_GROUPS = 8
SG_CHUNK = 128

VMEM_LIMIT_BYTES = 56 * 1024 * 1024

BF16 = jnp.bfloat16
F32 = jnp.float32


def _mm(a, b):
    return jnp.dot(a.astype(BF16), b.astype(BF16), preferred_element_type=F32)


def _mm_nt(a, b):
    return lax.dot_general(a.astype(BF16), b.astype(BF16), (((1,), (1,)), ((), ())),
                           preferred_element_type=F32)


def _mm_tn(a, b):
    return lax.dot_general(a.astype(BF16), b.astype(BF16), (((0,), (0,)), ((), ())),
                           preferred_element_type=F32)


def _rms(x, g):
    return x * lax.rsqrt(jnp.mean(x * x, axis=-1, keepdims=True) + RMS_EPS) * g


def _silu(x):
    return x * jax.nn.sigmoid(x)


def _const_spec(shape):
    nd = len(shape)
    return pl.BlockSpec(shape, lambda *_: (0,) * nd, pipeline_mode=pl.Buffered(1))


def _params(sem):
    return pltpu.CompilerParams(dimension_semantics=sem, vmem_limit_bytes=VMEM_LIMIT_BYTES)


def _ffn_kernel(x_ref, gpre_ref, wg_ref, wu_ref, wd_ref, gpost_ref, o_ref):
    x = x_ref[...]
    h = _rms(x, gpre_ref[...]).astype(BF16)
    gate = jnp.dot(h, wg_ref[...], preferred_element_type=F32)
    up = jnp.dot(h, wu_ref[...], preferred_element_type=F32)
    a = (_silu(gate) * up).astype(BF16)
    y = jnp.dot(a, wd_ref[...], preferred_element_type=F32)
    o_ref[...] = x + 0.5 * _rms(y, gpost_ref[...])


def _ffn(x2, gpre, wg, wu, wd, gpost, tm=512):
    T, D = x2.shape
    F = wg.shape[1]
    return pl.pallas_call(
        _ffn_kernel,
        grid=(T // tm,),
        in_specs=[pl.BlockSpec((tm, D), lambda i: (i, 0)),
                  _const_spec((1, D)), _const_spec((D, F)), _const_spec((D, F)),
                  _const_spec((F, D)), _const_spec((1, D))],
        out_specs=pl.BlockSpec((tm, D), lambda i: (i, 0)),
        out_shape=jax.ShapeDtypeStruct((T, D), F32),
        compiler_params=_params(("parallel",)),
        name="ffn",
    )(x2, gpre, wg, wu, wd, gpost)


def _dn_in_kernel(x_ref, gpre_ref, wqkv_ref, wba_ref, wbat_ref, conv_ref, alog_c_ref, dtb_c_ref,
                  alog_r_ref, dtb_r_ref, q_ref, k_ref, v_ref, gcol_ref, grow_ref, ext_ref, *, tm):
    H, Dh, C = DN_HEADS, DN_HEAD_DIM, DN_CHUNK
    W = H * Dh
    j = pl.program_id(1)
    x = x_ref[0]
    h = _rms(x, gpre_ref[...]).astype(BF16)
    proj = jnp.dot(h, wqkv_ref[...], preferred_element_type=F32)

    @pl.when(j == 0)
    def _():
        ext_ref[0:8, :] = jnp.zeros((8, 3 * W), F32)

    ext_ref[8:8 + tm, :] = proj
    cw = conv_ref[...]
    acc = proj * cw[DN_CONV - 1:DN_CONV, :]
    for s in range(1, DN_CONV):
        acc = acc + ext_ref[8 - s:8 - s + tm, :] * cw[DN_CONV - 1 - s:DN_CONV - s, :]
    ext_ref[0:8, :] = ext_ref[tm:tm + 8, :]
    qkv = _silu(acc)

    scale = Dh ** -0.5
    for hd in range(H):
        qh = qkv[:, hd * Dh:(hd + 1) * Dh]
        kh = qkv[:, W + hd * Dh:W + (hd + 1) * Dh]
        q_ref[0, :, hd * Dh:(hd + 1) * Dh] = qh * (lax.rsqrt(jnp.sum(qh * qh, -1, keepdims=True) + L2_EPS) * scale)
        k_ref[0, :, hd * Dh:(hd + 1) * Dh] = kh * lax.rsqrt(jnp.sum(kh * kh, -1, keepdims=True) + L2_EPS)
    v_ref[0] = qkv[:, 2 * W:]

    ba = jnp.dot(h, wba_ref[...], preferred_element_type=F32)
    bat = lax.dot_general(wbat_ref[...], h, (((1,), (1,)), ((), ())),
                          preferred_element_type=F32)
    beta_c = jax.nn.sigmoid(ba[:, :H])
    g_c = -jnp.exp(alog_c_ref[...]) * jax.nn.softplus(ba[:, H:] + dtb_c_ref[...])
    g_r = -jnp.exp(alog_r_ref[...]) * jax.nn.softplus(bat[H:, :] + dtb_r_ref[...])
    ri = lax.broadcasted_iota(jnp.int32, (tm, tm), 0)
    ci = lax.broadcasted_iota(jnp.int32, (tm, tm), 1)
    same = (ri // C) == (ci // C)
    tril = jnp.where(same & (ci <= ri), 1.0, 0.0).astype(F32)
    triu = jnp.where(same & (ri <= ci), 1.0, 0.0).astype(F32)
    gc_c = jnp.dot(tril, g_c, precision=lax.Precision.HIGHEST, preferred_element_type=F32)
    gc_r = jnp.dot(g_r, triu, precision=lax.Precision.HIGHEST, preferred_element_type=F32)
    gcol_ref[0, :, 0:H] = beta_c
    gcol_ref[0, :, H:2 * H] = gc_c
    for c in range(tm // C):
        grow_ref[0, c] = gc_r[:, c * C:(c + 1) * C]


def _dn_in(x, gpre, wqkv, wba, wbat, conv_w, alog, dtb, tm=256):
    B, S, D = x.shape
    H, C = DN_HEADS, DN_CHUNK
    W = H * DN_HEAD_DIM
    tok = lambda b, j: (b, j, 0)
    return pl.pallas_call(
        functools.partial(_dn_in_kernel, tm=tm),
        grid=(B, S // tm),
        in_specs=[pl.BlockSpec((1, tm, D), tok),
                  _const_spec((1, D)), _const_spec((D, 3 * W)), _const_spec((D, 2 * H)),
                  _const_spec((2 * H, D)), _const_spec((DN_CONV, 3 * W)),
                  _const_spec((1, H)), _const_spec((1, H)), _const_spec((H, 1)), _const_spec((H, 1))],
        out_specs=[pl.BlockSpec((1, tm, W), tok), pl.BlockSpec((1, tm, W), tok),
                   pl.BlockSpec((1, tm, W), tok), pl.BlockSpec((1, tm, 2 * H), tok),
                   pl.BlockSpec((1, tm // C, H, C), lambda b, j: (b, j, 0, 0))],
        out_shape=[jax.ShapeDtypeStruct((B, S, W), F32), jax.ShapeDtypeStruct((B, S, W), F32),
                   jax.ShapeDtypeStruct((B, S, W), F32), jax.ShapeDtypeStruct((B, S, 2 * H), F32),
                   jax.ShapeDtypeStruct((B, S // C, H, C), F32)],
        scratch_shapes=[pltpu.VMEM((tm + 8, 3 * W), F32)],
        compiler_params=_params(("arbitrary", "arbitrary")),
        name="dn_in",
    )(x, gpre, wqkv, wba, wbat, conv_w, alog.reshape(1, H), dtb.reshape(1, H),
      alog.reshape(H, 1), dtb.reshape(H, 1))


def _unit_lower_inverse(low, ri, ci):
    C = low.shape[0]
    eye = jnp.where(ri == ci, 1.0, 0.0).astype(F32)
    t = eye - jnp.where((ri // 2) == (ci // 2), low, 0.0)
    b = 2
    while b < C:
        off = jnp.where(((ri // (2 * b)) == (ci // (2 * b))) & ((ri // b) != (ci // b)), low, 0.0)
        t = t - _mm(t, _mm(off, t))
        b *= 2
    return t


def _dn_delta_kernel(q_ref, k_ref, v_ref, gcol_ref, grow_ref, o_ref, s_ref, *, tt):
    H, Dh, C = DN_HEADS, DN_HEAD_DIM, DN_CHUNK

    @pl.when(pl.program_id(1) == 0)
    def _():
        s_ref[...] = jnp.zeros_like(s_ref)

    ri = lax.broadcasted_iota(jnp.int32, (C, C), 0)
    ci = lax.broadcasted_iota(jnp.int32, (C, C), 1)
    causal = ci <= ri
    strict = ci < ri

    def chunk(c, carry):
        r0 = pl.multiple_of(c * C, C)
        gcol = gcol_ref[0, pl.ds(r0, C), :]
        grow = grow_ref[0, c]
        for hd in range(H):
            cols = slice(hd * Dh, (hd + 1) * Dh)
            q = q_ref[0, pl.ds(r0, C), cols]
            k = k_ref[0, pl.ds(r0, C), cols]
            v = v_ref[0, pl.ds(r0, C), cols]
            beta = gcol[:, hd:hd + 1]
            gc = gcol[:, H + hd:H + hd + 1]
            gr = grow[hd:hd + 1, :]
            gl = gc[C - 1:C, :]
            decay = jnp.where(causal, jnp.exp(jnp.where(causal, gc - gr, 0.0)), 0.0)
            egc = jnp.exp(gc)
            kb = k * beta
            low = jnp.where(strict, _mm_nt(kb, k) * decay, 0.0)
            t = _unit_lower_inverse(low, ri, ci)
            u = _mm(t, v * beta)
            w = _mm(t, kb * egc)
            attn = _mm_nt(q, k) * decay
            state = s_ref[hd]
            v_new = u - _mm(w, state)
            o_ref[0, pl.ds(r0, C), cols] = _mm(q * egc, state) + _mm(attn, v_new)
            s_ref[hd] = state * jnp.exp(gl) + _mm_tn(k * jnp.exp(gl - gc), v_new)
        return carry

    lax.fori_loop(0, tt // C, chunk, 0)


def _dn_delta(q, k, v, gcol, grow, tt=256):
    B, S, W = q.shape
    H, Dh, C = DN_HEADS, DN_HEAD_DIM, DN_CHUNK
    tok = lambda b, j: (b, j, 0)
    return pl.pallas_call(
        functools.partial(_dn_delta_kernel, tt=tt),
        grid=(B, S // tt),
        in_specs=[pl.BlockSpec((1, tt, W), tok), pl.BlockSpec((1, tt, W), tok),
                  pl.BlockSpec((1, tt, W), tok), pl.BlockSpec((1, tt, 2 * H), tok),
                  pl.BlockSpec((1, tt // C, H, C), lambda b, j: (b, j, 0, 0))],
        out_specs=pl.BlockSpec((1, tt, W), tok),
        out_shape=jax.ShapeDtypeStruct((B, S, W), F32),
        scratch_shapes=[pltpu.VMEM((H, Dh, Dh), F32)],
        compiler_params=_params(("arbitrary", "arbitrary")),
        name="dn_delta",
    )(q, k, v, gcol, grow)


def _dn_out_kernel(x_ref, o_ref, gpre_ref, wz_ref, ng_ref, wout_ref, gpost_ref, y_ref):
    H, Dh = DN_HEADS, DN_HEAD_DIM
    x = x_ref[...]
    h = _rms(x, gpre_ref[...]).astype(BF16)
    z = jnp.dot(h, wz_ref[...], preferred_element_type=F32)
    ng = ng_ref[...]
    parts = []
    for hd in range(H):
        oh = o_ref[:, hd * Dh:(hd + 1) * Dh]
        parts.append((_rms(oh, ng) * _silu(z[:, hd * Dh:(hd + 1) * Dh])).astype(BF16))
    gated = jnp.concatenate(parts, axis=-1)
    m = jnp.dot(gated, wout_ref[...], preferred_element_type=F32)
    y_ref[...] = x + _rms(m, gpost_ref[...])


def _dn_out(x2, o2, gpre, wz, ng, wout, gpost, tm=512):
    T, D = x2.shape
    W = o2.shape[1]
    row = lambda i: (i, 0)
    return pl.pallas_call(
        _dn_out_kernel,
        grid=(T // tm,),
        in_specs=[pl.BlockSpec((tm, D), row), pl.BlockSpec((tm, W), row),
                  _const_spec((1, D)), _const_spec((D, W)), _const_spec((1, DN_HEAD_DIM)),
                  _const_spec((W, D)), _const_spec((1, D))],
        out_specs=pl.BlockSpec((tm, D), row),
        out_shape=jax.ShapeDtypeStruct((T, D), F32),
        compiler_params=_params(("parallel",)),
        name="dn_out",
    )(x2, o2, gpre, wz, ng, wout, gpost)


def _sg_kernel(x_ref, gpre_ref, win_ref, bin_ref, lng_ref, lnb_ref, ws_ref, bst_ref, wout_ref,
               gpost_ref, y_ref, *, tm):
    G, C = SG_GROUPS, SG_CHUNK
    x = x_ref[...]
    h = _rms(x, gpre_ref[...]).astype(BF16)
    zz = jnp.dot(h, win_ref[...], preferred_element_type=F32) + bin_ref[...]
    zz = 0.5 * zz * (1.0 + lax.erf(zz * (2.0 ** -0.5)))
    E = zz.shape[1] // 2
    Eg = E // G
    u = zz[:, :E]
    vv = zz[:, E:]
    mu = jnp.mean(vv, axis=-1, keepdims=True)
    xc = vv - mu
    vn = (xc * lax.rsqrt(jnp.mean(xc * xc, axis=-1, keepdims=True) + LN_EPS) * lng_ref[...]
          + lnb_ref[...]).astype(BF16)
    ri = lax.broadcasted_iota(jnp.int32, (C, C), 0)
    ci = lax.broadcasted_iota(jnp.int32, (C, C), 1)
    bst = bst_ref[...]
    rows = []
    for c in range(tm // C):
        parts = []
        for g in range(G):
            wc = jnp.where(ci <= ri, ws_ref[g], 0.0).astype(BF16)
            mixed = jnp.dot(wc, vn[c * C:(c + 1) * C, g * Eg:(g + 1) * Eg],
                            preferred_element_type=F32) + bst[:, g:g + 1]
            parts.append((u[c * C:(c + 1) * C, g * Eg:(g + 1) * Eg] * mixed).astype(BF16))
        rows.append(jnp.concatenate(parts, axis=-1))
    gated = jnp.concatenate(rows, axis=0)
    m = jnp.dot(gated, wout_ref[...], preferred_element_type=F32)
    y_ref[...] = x + _rms(m, gpost_ref[...])


def _sg(x2, gpre, win, b_in, lng, lnb, ws, bst, wout, gpost, tm=256):
    T, D = x2.shape
    E2 = win.shape[1]
    E = E2 // 2
    G, C = SG_GROUPS, SG_CHUNK
    row = lambda i: (i, 0)
    return pl.pallas_call(
        functools.partial(_sg_kernel, tm=tm),
        grid=(T // tm,),
        in_specs=[pl.BlockSpec((tm, D), row), _const_spec((1, D)), _const_spec((D, E2)),
                  _const_spec((1, E2)), _const_spec((1, E)), _const_spec((1, E)),
                  _const_spec((G, C, C)), _const_spec((C, G)), _const_spec((E, D)),
                  _const_spec((1, D))],
        out_specs=pl.BlockSpec((tm, D), row),
        out_shape=jax.ShapeDtypeStruct((T, D), F32),
        compiler_params=_params(("parallel",)),
        name="sg",
    )(x2, gpre, win, b_in, lng, lnb, ws, bst, wout, gpost)


def kernel(x, norm_g, ffn_w_gate, ffn_w_up, ffn_w_down, dn_w_in, dn_conv_w, dn_a_log, dn_dt_bias,
           dn_norm_g, dn_w_out, sg_w_in, sg_b_in, sg_ln_g, sg_ln_b, sg_w_s, sg_b_s, sg_w_out):
    B, S, D = x.shape
    depth = norm_g.shape[0]
    H = DN_HEADS
    W = H * DN_HEAD_DIM
    wg = ffn_w_gate.astype(BF16)
    wu = ffn_w_up.astype(BF16)
    wd = ffn_w_down.astype(BF16)
    x2 = x.reshape(B * S, D)
    for i in range(depth):
        ng = norm_g[i].reshape(6, 1, D)
        x2 = _ffn(x2, ng[0], wg[i, 0], wu[i, 0], wd[i, 0], ng[1])
        j = i // 2
        if i % 2 == 0:
            w_in = dn_w_in[j].astype(BF16)
            q, k, v, gcol, grow = _dn_in(x2.reshape(B, S, D), ng[2], w_in[:, :3 * W],
                                         w_in[:, 4 * W:], w_in[:, 4 * W:].T, dn_conv_w[j],
                                         dn_a_log[j], dn_dt_bias[j])
            o = _dn_delta(q, k, v, gcol, grow)
            x2 = _dn_out(x2, o.reshape(B * S, W), ng[2], w_in[:, 3 * W:4 * W],
                         dn_norm_g[j].reshape(1, -1), dn_w_out[j].astype(BF16), ng[3])
        else:
            E = sg_w_in.shape[2] // 2
            x2 = _sg(x2, ng[2], sg_w_in[j].astype(BF16), sg_b_in[j].reshape(1, -1),
                     sg_ln_g[j].reshape(1, E), sg_ln_b[j].reshape(1, E), sg_w_s[j], sg_b_s[j].T,
                     sg_w_out[j].astype(BF16), ng[3])
        x2 = _ffn(x2, ng[4], wg[i, 1], wu[i, 1], wd[i, 1], ng[5])
    return x2.reshape(B, S, D)
```

```python
import functools
import math

import jax
import jax.numpy as jnp
from jax import lax
from jax.experimental import pallas as pl
from jax.experimental.pallas import tpu as pltpu

RMS_EPS = 1e-6
LN_EPS = 1e-5
L2_EPS = 1e-6
DN_HEADS = 8
DN_HEAD_DIM = 128
DN_CONV = 4
DN_CHUNK = 64
SG_GROUPS = 8
SG_CHUNK = 128

VMEM_LIMIT_BYTES = 56 * 1024 * 1024

BF16 = jnp.bfloat16
F32 = jnp.float32


def _mm(a, b):
    return jnp.dot(a.astype(BF16), b.astype(BF16), preferred_element_type=F32)


def _mm_nt(a, b):
    return lax.dot_general(a.astype(BF16), b.astype(BF16), (((1,), (1,)), ((), ())),
                           preferred_element_type=F32)


def _mm_tn(a, b):
    return lax.dot_general(a.astype(BF16), b.astype(BF16), (((0,), (0,)), ((), ())),
                           preferred_element_type=F32)


def _rms(x, g):
    return x * lax.rsqrt(jnp.mean(x * x, axis=-1, keepdims=True) + RMS_EPS) * g


def _silu(x):
    return x * jax.nn.sigmoid(x)


def _const_spec(shape):
    nd = len(shape)
    return pl.BlockSpec(shape, lambda *_: (0,) * nd, pipeline_mode=pl.Buffered(1))


def _params(sem):
    return pltpu.CompilerParams(dimension_semantics=sem, vmem_limit_bytes=VMEM_LIMIT_BYTES)


def _ffn_kernel(x_ref, gpre_ref, wg_ref, wu_ref, wd_ref, gpost_ref, o_ref):
    x = x_ref[...]
    h = _rms(x, gpre_ref[...]).astype(BF16)
    gate = jnp.dot(h, wg_ref[...], preferred_element_type=F32)
    up = jnp.dot(h, wu_ref[...], preferred_element_type=F32)
    a = (_silu(gate) * up).astype(BF16)
    y = jnp.dot(a, wd_ref[...], preferred_element_type=F32)
    o_ref[...] = x + 0.5 * _rms(y, gpost_ref[...])


def _ffn(x2, gpre, wg, wu, wd, gpost, tm=512):
    T, D = x2.shape
    F = wg.shape[1]
    return pl.pallas_call(
        _ffn_kernel,
        grid=(T // tm,),
        in_specs=[pl.BlockSpec((tm, D), lambda i: (i, 0)),
                  _const_spec((1, D)), _const_spec((D, F)), _const_spec((D, F)),
                  _const_spec((F, D)), _const_spec((1, D))],
        out_specs=pl.BlockSpec((tm, D), lambda i: (i, 0)),
        out_shape=jax.ShapeDtypeStruct((T, D), F32),
        compiler_params=_params(("parallel",)),
        name="ffn",
    )(x2, gpre, wg, wu, wd, gpost)


def _dn_in_kernel(x_ref, gpre_ref, wqkv_ref, wba_ref, wbat_ref, conv_ref, alog_c_ref, dtb_c_ref,
                  alog_r_ref, dtb_r_ref, q_ref, k_ref, v_ref, gcol_ref, grow_ref, ext_ref, *, tm):
    H, Dh, C = DN_HEADS, DN_HEAD_DIM, DN_CHUNK
    W = H * Dh
    j = pl.program_id(1)
    x = x_ref[0]
    h = _rms(x, gpre_ref[...]).astype(BF16)
    proj = jnp.dot(h, wqkv_ref[...], preferred_element_type=F32)

    @pl.when(j == 0)
    def _():
        ext_ref[0:8, :] = jnp.zeros((8, 3 * W), F32)

    ext_ref[8:8 + tm, :] = proj
    cw = conv_ref[...]
    acc = proj * cw[DN_CONV - 1:DN_CONV, :]
    for s in range(1, DN_CONV):
        acc = acc + ext_ref[8 - s:8 - s + tm, :] * cw[DN_CONV - 1 - s:DN_CONV - s, :]
    ext_ref[0:8, :] = ext_ref[tm:tm + 8, :]
    qkv = _silu(acc)

    scale = Dh ** -0.5
    for hd in range(H):
        qh = qkv[:, hd * Dh:(hd + 1) * Dh]
        kh = qkv[:, W + hd * Dh:W + (hd + 1) * Dh]
        q_ref[0, :, hd * Dh:(hd + 1) * Dh] = qh * (lax.rsqrt(jnp.sum(qh * qh, -1, keepdims=True) + L2_EPS) * scale)
        k_ref[0, :, hd * Dh:(hd + 1) * Dh] = kh * lax.rsqrt(jnp.sum(kh * kh, -1, keepdims=True) + L2_EPS)
    v_ref[0] = qkv[:, 2 * W:]

    ba = jnp.dot(h, wba_ref[...], preferred_element_type=F32)
    bat = lax.dot_general(wbat_ref[...], h, (((1,), (1,)), ((), ())),
                          preferred_element_type=F32)
    beta_c = jax.nn.sigmoid(ba[:, :H])
    g_c = -jnp.exp(alog_c_ref[...]) * jax.nn.softplus(ba[:, H:] + dtb_c_ref[...])
    g_r = -jnp.exp(alog_r_ref[...]) * jax.nn.softplus(bat[H:, :] + dtb_r_ref[...])
    ri = lax.broadcasted_iota(jnp.int32, (tm, tm), 0)
    ci = lax.broadcasted_iota(jnp.int32, (tm, tm), 1)
    same = (ri // C) == (ci // C)
    tril = jnp.where(same & (ci <= ri), 1.0, 0.0).astype(F32)
    triu = jnp.where(same & (ri <= ci), 1.0, 0.0).astype(F32)
    gc_c = jnp.dot(tril, g_c, precision=lax.Precision.HIGHEST, preferred_element_type=F32)
    gc_r = jnp.dot(g_r, triu, precision=lax.Precision.HIGHEST, preferred_element_type=F32)
    gcol_ref[0, :, 0:H] = beta_c
    gcol_ref[0, :, H:2 * H] = gc_c
    for c in range(tm // C):
        grow_ref[0, c] = gc_r[:, c * C:(c + 1) * C]


def _dn_in(x, gpre, wqkv, wba, wbat, conv_w, alog, dtb, tm=256):
    B, S, D = x.shape
    H, C = DN_HEADS, DN_CHUNK
    W = H * DN_HEAD_DIM
    tok = lambda b, j: (b, j, 0)
    return pl.pallas_call(
        functools.partial(_dn_in_kernel, tm=tm),
        grid=(B, S // tm),
        in_specs=[pl.BlockSpec((1, tm, D), tok),
                  _const_spec((1, D)), _const_spec((D, 3 * W)), _const_spec((D, 2 * H)),
                  _const_spec((2 * H, D)), _const_spec((DN_CONV, 3 * W)),
                  _const_spec((1, H)), _const_spec((1, H)), _const_spec((H, 1)), _const_spec((H, 1))],
        out_specs=[pl.BlockSpec((1, tm, W), tok), pl.BlockSpec((1, tm, W), tok),
                   pl.BlockSpec((1, tm, W), tok), pl.BlockSpec((1, tm, 2 * H), tok),
                   pl.BlockSpec((1, tm // C, H, C), lambda b, j: (b, j, 0, 0))],
        out_shape=[jax.ShapeDtypeStruct((B, S, W), F32), jax.ShapeDtypeStruct((B, S, W), F32),
                   jax.ShapeDtypeStruct((B, S, W), F32), jax.ShapeDtypeStruct((B, S, 2 * H), F32),
                   jax.ShapeDtypeStruct((B, S // C, H, C), F32)],
        scratch_shapes=[pltpu.VMEM((tm + 8, 3 * W), F32)],
        compiler_params=_params(("arbitrary", "arbitrary")),
        name="dn_in",
    )(x, gpre, wqkv, wba, wbat, conv_w, alog.reshape(1, H), dtb.reshape(1, H),
      alog.reshape(H, 1), dtb.reshape(H, 1))


def _bmm(a, b):
    return lax.dot_general(a.astype(BF16), b.astype(BF16), (((2,), (1,)), ((0,), (0,))),
                           preferred_element_type=F32)


def _bmm_nt(a, b):
    return lax.dot_general(a.astype(BF16), b.astype(BF16), (((2,), (2,)), ((0,), (0,))),
                           preferred_element_type=F32)


def _bmm_tn(a, b):
    return lax.dot_general(a.astype(BF16), b.astype(BF16), (((1,), (1,)), ((0,), (0,))),
                           preferred_element_type=F32)


def _unit_lower_inverse(low, ri, ci):
    C = low.shape[-1]
    eye = jnp.where(ri == ci, 1.0, 0.0).astype(F32)
    t = eye - jnp.where((ri // 2) == (ci // 2), low, 0.0)
    b = 2
    while b < C:
        off = jnp.where(((ri // (2 * b)) == (ci // (2 * b))) & ((ri // b) != (ci // b)), low, 0.0)
        t = t - _bmm(t, _bmm(off, t))
        b *= 2
    return t


def _dn_delta_kernel(q_ref, k_ref, v_ref, gcol_ref, grow_ref, o_ref, s_ref, *, tt):
    H, Dh, C = DN_HEADS, DN_HEAD_DIM, DN_CHUNK
    nc = tt // C

    @pl.when(pl.program_id(1) == 0)
    def _():
        s_ref[...] = jnp.zeros_like(s_ref)

    ri = lax.broadcasted_iota(jnp.int32, (1, C, C), 1)
    ci = lax.broadcasted_iota(jnp.int32, (1, C, C), 2)
    causal = ci <= ri
    strict = ci < ri

    def per_head(ref):
        return jnp.stack([ref[0, c * C:(c + 1) * C, hd * Dh:(hd + 1) * Dh]
                          for c in range(nc) for hd in range(H)])
    q, k, v = per_head(q_ref), per_head(k_ref), per_head(v_ref)
    gcol = gcol_ref[0]
    beta = jnp.stack([gcol[c * C:(c + 1) * C, hd:hd + 1] for c in range(nc) for hd in range(H)])
    gc = jnp.stack([gcol[c * C:(c + 1) * C, H + hd:H + hd + 1]
                    for c in range(nc) for hd in range(H)])
    gr = jnp.stack([grow_ref[0, c, hd:hd + 1, :] for c in range(nc) for hd in range(H)])
    gl = gc[:, C - 1:C, :]
    decay = jnp.where(causal, jnp.exp(jnp.where(causal, gc - gr, 0.0)), 0.0)
    egc = jnp.exp(gc)
    kb = k * beta
    low = jnp.where(strict, _bmm_nt(kb, k) * decay, 0.0)
    t = _unit_lower_inverse(low, ri, ci)
    u = _bmm(t, v * beta)
    w = _bmm(t, kb * egc)
    attn = _bmm_nt(q, k) * decay
    qd = q * egc
    kd = k * jnp.exp(gl - gc)
    egl = jnp.exp(gl)

    state = s_ref[...]
    for c in range(nc):
        sl = slice(c * H, (c + 1) * H)
        v_new = u[sl] - _bmm(w[sl], state)
        o = _bmm(qd[sl], state) + _bmm(attn[sl], v_new)
        for hd in range(H):
            o_ref[0, c * C:(c + 1) * C, hd * Dh:(hd + 1) * Dh] = o[hd]
        state = state * egl[sl] + _bmm_tn(kd[sl], v_new)
    s_ref[...] = state


def _dn_delta(q, k, v, gcol, grow, tt=256):
    B, S, W = q.shape
    H, Dh, C = DN_HEADS, DN_HEAD_DIM, DN_CHUNK
    tok = lambda b, j: (b, j, 0)
    return pl.pallas_call(
        functools.partial(_dn_delta_kernel, tt=tt),
        grid=(B, S // tt),
        in_specs=[pl.BlockSpec((1, tt, W), tok), pl.BlockSpec((1, tt, W), tok),
                  pl.BlockSpec((1, tt, W), tok), pl.BlockSpec((1, tt, 2 * H), tok),
                  pl.BlockSpec((1, tt // C, H, C), lambda b, j: (b, j, 0, 0))],
        out_specs=pl.BlockSpec((1, tt, W), tok),
        out_shape=jax.ShapeDtypeStruct((B, S, W), F32),
        scratch_shapes=[pltpu.VMEM((H, Dh, Dh), F32)],
        compiler_params=_params(("arbitrary", "arbitrary")),
        name="dn_delta",
    )(q, k, v, gcol, grow)


def _dn_out_kernel(x_ref, o_ref, gpre_ref, wz_ref, ng_ref, wout_ref, gpost_ref, y_ref):
    H, Dh = DN_HEADS, DN_HEAD_DIM
    x = x_ref[...]
    h = _rms(x, gpre_ref[...]).astype(BF16)
    z = jnp.dot(h, wz_ref[...], preferred_element_type=F32)
    ng = ng_ref[...]
    parts = []
    for hd in range(H):
        oh = o_ref[:, hd * Dh:(hd + 1) * Dh]
        parts.append((_rms(oh, ng) * _silu(z[:, hd * Dh:(hd + 1) * Dh])).astype(BF16))
    gated = jnp.concatenate(parts, axis=-1)
    m = jnp.dot(gated, wout_ref[...], preferred_element_type=F32)
    y_ref[...] = x + _rms(m, gpost_ref[...])


def _dn_out(x2, o2, gpre, wz, ng, wout, gpost, tm=512):
    T, D = x2.shape
    W = o2.shape[1]
    row = lambda i: (i, 0)
    return pl.pallas_call(
        _dn_out_kernel,
        grid=(T // tm,),
        in_specs=[pl.BlockSpec((tm, D), row), pl.BlockSpec((tm, W), row),
                  _const_spec((1, D)), _const_spec((D, W)), _const_spec((1, DN_HEAD_DIM)),
                  _const_spec((W, D)), _const_spec((1, D))],
        out_specs=pl.BlockSpec((tm, D), row),
        out_shape=jax.ShapeDtypeStruct((T, D), F32),
        compiler_params=_params(("parallel",)),
        name="dn_out",
    )(x2, o2, gpre, wz, ng, wout, gpost)


def _sg_kernel(x_ref, gpre_ref, win_ref, bin_ref, lng_ref, lnb_ref, ws_ref, bst_ref, wout_ref,
               gpost_ref, y_ref, *, tm):
    G, C = SG_GROUPS, SG_CHUNK
    x = x_ref[...]
    h = _rms(x, gpre_ref[...]).astype(BF16)
    zz = jnp.dot(h, win_ref[...], preferred_element_type=F32) + bin_ref[...]
    zz = 0.5 * zz * (1.0 + lax.erf(zz * (2.0 ** -0.5)))
    E = zz.shape[1] // 2
    Eg = E // G
    u = zz[:, :E]
    vv = zz[:, E:]
    mu = jnp.mean(vv, axis=-1, keepdims=True)
    xc = vv - mu
    vn = (xc * lax.rsqrt(jnp.mean(xc * xc, axis=-1, keepdims=True) + LN_EPS) * lng_ref[...]
          + lnb_ref[...]).astype(BF16)
    ri = lax.broadcasted_iota(jnp.int32, (C, C), 0)
    ci = lax.broadcasted_iota(jnp.int32, (C, C), 1)
    bst = bst_ref[...]
    rows = []
    for c in range(tm // C):
        parts = []
        for g in range(G):
            wc = jnp.where(ci <= ri, ws_ref[g], 0.0).astype(BF16)
            mixed = jnp.dot(wc, vn[c * C:(c + 1) * C, g * Eg:(g + 1) * Eg],
                            preferred_element_type=F32) + bst[:, g:g + 1]
            parts.append((u[c * C:(c + 1) * C, g * Eg:(g + 1) * Eg] * mixed).astype(BF16))
        rows.append(jnp.concatenate(parts, axis=-1))
    gated = jnp.concatenate(rows, axis=0)
    m = jnp.dot(gated, wout_ref[...], preferred_element_type=F32)
    y_ref[...] = x + _rms(m, gpost_ref[...])


def _sg(x2, gpre, win, b_in, lng, lnb, ws, bst, wout, gpost, tm=256):
    T, D = x2.shape
    E2 = win.shape[1]
    E = E2 // 2
    G, C = SG_GROUPS, SG_CHUNK
    row = lambda i: (i, 0)
    return pl.pallas_call(
        functools.partial(_sg_kernel, tm=tm),
        grid=(T // tm,),
        in_specs=[pl.BlockSpec((tm, D), row), _const_spec((1, D)), _const_spec((D, E2)),
                  _const_spec((1, E2)), _const_spec((1, E)), _const_spec((1, E)),
                  _const_spec((G, C, C)), _const_spec((C, G)), _const_spec((E, D)),
                  _const_spec((1, D))],
        out_specs=pl.BlockSpec((tm, D), row),
        out_shape=jax.ShapeDtypeStruct((T, D), F32),
        compiler_params=_params(("parallel",)),
        name="sg",
    )(x2, gpre, win, b_in, lng, lnb, ws, bst, wout, gpost)


def kernel(x, norm_g, ffn_w_gate, ffn_w_up, ffn_w_down, dn_w_in, dn_conv_w, dn_a_log, dn_dt_bias,
           dn_norm_g, dn_w_out, sg_w_in, sg_b_in, sg_ln_g, sg_ln_b, sg_w_s, sg_b_s, sg_w_out):
    B, S, D = x.shape
    depth = norm_g.shape[0]
    H = DN_HEADS
    W = H * DN_HEAD_DIM
    wg = ffn_w_gate.astype(BF16)
    wu = ffn_w_up.astype(BF16)
    wd = ffn_w_down.astype(BF16)
    x2 = x.reshape(B * S, D)
    for i in range(depth):
        ng = norm_g[i].reshape(6, 1, D)
        x2 = _ffn(x2, ng[0], wg[i, 0], wu[i, 0], wd[i, 0], ng[1])
        j = i // 2
        if i % 2 == 0:
            w_in = dn_w_in[j].astype(BF16)
            q, k, v, gcol, grow = _dn_in(x2.reshape(B, S, D), ng[2], w_in[:, :3 * W],
                                         w_in[:, 4 * W:], w_in[:, 4 * W:].T, dn_conv_w[j],
                                         dn_a_log[j], dn_dt_bias[j])
            o = _dn_delta(q, k, v, gcol, grow)
            x2 = _dn_out(x2, o.reshape(B * S, W), ng[2], w_in[:, 3 * W:4 * W],
                         dn_norm_g[j].reshape(1, -1), dn_w_out[j].astype(BF16), ng[3])
        else:
            E = sg_w_in.shape[2] // 2
            x2 = _sg(x2, ng[2], sg_w_in[j].astype(BF16), sg_b_in[j].reshape(1, -1),
                     sg_ln_g[j].reshape(1, E), sg_ln_b[j].reshape(1, E), sg_w_s[j], sg_b_s[j].T,
                     sg_w_out[j].astype(BF16), ng[3])
        x2 = _ffn(x2, ng[4], wg[i, 1], wu[i, 1], wd[i, 1], ng[5])
    return x2.reshape(B, S, D)
```

```python
import functools
import math

import jax
import jax.numpy as jnp
from jax import lax
from jax.experimental import pallas as pl
from jax.experimental.pallas import tpu as pltpu

RMS_EPS = 1e-6
LN_EPS = 1e-5
L2_EPS = 1e-6
DN_HEADS = 8
DN_HEAD_DIM = 128
DN_CONV = 4
DN_CHUNK = 64
SG_GROUPS = 8
SG_CHUNK = 128

VMEM_LIMIT_BYTES = 56 * 1024 * 1024

BF16 = jnp.bfloat16
F32 = jnp.float32


def _mm(a, b):
    return jnp.dot(a.astype(BF16), b.astype(BF16), preferred_element_type=F32)


def _mm_nt(a, b):
    return lax.dot_general(a.astype(BF16), b.astype(BF16), (((1,), (1,)), ((), ())),
                           preferred_element_type=F32)


def _mm_tn(a, b):
    return lax.dot_general(a.astype(BF16), b.astype(BF16), (((0,), (0,)), ((), ())),
                           preferred_element_type=F32)


def _rms(x, g):
    return x * lax.rsqrt(jnp.mean(x * x, axis=-1, keepdims=True) + RMS_EPS) * g


def _silu(x):
    return x * jax.nn.sigmoid(x)


def _const_spec(shape):
    nd = len(shape)
    return pl.BlockSpec(shape, lambda *_: (0,) * nd, pipeline_mode=pl.Buffered(1))


def _params(sem):
    return pltpu.CompilerParams(dimension_semantics=sem, vmem_limit_bytes=VMEM_LIMIT_BYTES)


def _ffn_kernel(x_ref, gpre_ref, wg_ref, wu_ref, wd_ref, gpost_ref, o_ref):
    x = x_ref[...]
    h = _rms(x, gpre_ref[...]).astype(BF16)
    gate = jnp.dot(h, wg_ref[...], preferred_element_type=F32)
    up = jnp.dot(h, wu_ref[...], preferred_element_type=F32)
    a = (_silu(gate) * up).astype(BF16)
    y = jnp.dot(a, wd_ref[...], preferred_element_type=F32)
    o_ref[...] = x + 0.5 * _rms(y, gpost_ref[...])


def _ffn(x2, gpre, wg, wu, wd, gpost, tm=512):
    T, D = x2.shape
    F = wg.shape[1]
    return pl.pallas_call(
        _ffn_kernel,
        grid=(T // tm,),
        in_specs=[pl.BlockSpec((tm, D), lambda i: (i, 0)),
                  _const_spec((1, D)), _const_spec((D, F)), _const_spec((D, F)),
                  _const_spec((F, D)), _const_spec((1, D))],
        out_specs=pl.BlockSpec((tm, D), lambda i: (i, 0)),
        out_shape=jax.ShapeDtypeStruct((T, D), F32),
        compiler_params=_params(("parallel",)),
        name="ffn",
    )(x2, gpre, wg, wu, wd, gpost)


def _dn_in_kernel(x_ref, gpre_ref, wqkv_ref, wba_ref, wbat_ref, conv_ref, alog_c_ref, dtb_c_ref,
                  alog_r_ref, dtb_r_ref, seg_ref, segt_ref,
                  q_ref, k_ref, v_ref, gcol_ref, grow_ref, *ext_refs, tm):
    H, Dh, C = DN_HEADS, DN_HEAD_DIM, DN_CHUNK
    bw = 2 * Dh
    nb = H * Dh // bw

    @pl.when(pl.program_id(1) == 0)
    def _():
        for ext in ext_refs:
            ext[0:8, :] = jnp.zeros((8, bw), F32)

    x = x_ref[0]
    h = _rms(x, gpre_ref[...]).astype(BF16)
    n_blocks = len(ext_refs)

    def project(cb):
        cols = slice(cb * bw, (cb + 1) * bw)
        proj = jnp.dot(h, wqkv_ref[:, cols], preferred_element_type=F32)
        ext_refs[cb][8:8 + tm, :] = proj
        return proj

    projs = {0: project(0), 1: project(1)}
    for cb, ext in enumerate(ext_refs):
        cols = slice(cb * bw, (cb + 1) * bw)
        proj = projs.pop(cb)
        cw = conv_ref[:, cols]
        acc = proj * cw[DN_CONV - 1:DN_CONV, :]
        for s in range(1, DN_CONV):
            acc = acc + ext[8 - s:8 - s + tm, :] * cw[DN_CONV - 1 - s:DN_CONV - s, :]
        ext[0:8, :] = ext[tm:tm + 8, :]
        y = _silu(acc)
        normed = cb < 2 * nb
        if normed:
            ss = jnp.dot((y * y).astype(BF16), seg_ref[...], preferred_element_type=F32)
        if cb + 2 < n_blocks:
            projs[cb + 2] = project(cb + 2)
        if normed:
            inv = lax.rsqrt(ss + L2_EPS) * (Dh ** -0.5 if cb < nb else 1.0)
            hi = inv.astype(BF16)
            lo = (inv - hi.astype(F32)).astype(BF16)
            y = y * jnp.dot(jnp.concatenate([hi, lo], axis=1), segt_ref[...],
                            preferred_element_type=F32)
        out_ref = (q_ref, k_ref, v_ref)[cb // nb]
        out_ref[0, :, (cb % nb) * bw:(cb % nb + 1) * bw] = y

    ba = jnp.dot(h, wba_ref[...], preferred_element_type=F32)
    bat = lax.dot_general(wbat_ref[...], h, (((1,), (1,)), ((), ())),
                          preferred_element_type=F32)
    beta_c = jax.nn.sigmoid(ba[:, :H])
    g_c = -jnp.exp(alog_c_ref[...]) * jax.nn.softplus(ba[:, H:] + dtb_c_ref[...])
    g_r = -jnp.exp(alog_r_ref[...]) * jax.nn.softplus(bat[H:, :] + dtb_r_ref[...])
    ri = lax.broadcasted_iota(jnp.int32, (tm, tm), 0)
    ci = lax.broadcasted_iota(jnp.int32, (tm, tm), 1)
    same = (ri // C) == (ci // C)
    tril = jnp.where(same & (ci <= ri), 1.0, 0.0).astype(F32)
    triu = jnp.where(same & (ri <= ci), 1.0, 0.0).astype(F32)
    gc_c = jnp.dot(tril, g_c, precision=lax.Precision.HIGHEST, preferred_element_type=F32)
    gc_r = jnp.dot(g_r, triu, precision=lax.Precision.HIGHEST, preferred_element_type=F32)
    gcol_ref[0, :, 0:H] = beta_c
    gcol_ref[0, :, H:2 * H] = gc_c
    for c in range(tm // C):
        grow_ref[0, c] = gc_r[:, c * C:(c + 1) * C]


def _dn_in(x, gpre, wqkv, wba, wbat, conv_w, alog, dtb, tm=256):
    B, S, D = x.shape
    H, C = DN_HEADS, DN_CHUNK
    W = H * DN_HEAD_DIM
    tok = lambda b, j: (b, j, 0)
    bw = 2 * DN_HEAD_DIM
    seg = (jnp.arange(bw)[:, None] // DN_HEAD_DIM == jnp.arange(2)[None, :]).astype(BF16)
    segt = jnp.concatenate([seg.T, seg.T], axis=0)
    return pl.pallas_call(
        functools.partial(_dn_in_kernel, tm=tm),
        grid=(B, S // tm),
        in_specs=[pl.BlockSpec((1, tm, D), tok),
                  _const_spec((1, D)), _const_spec((D, 3 * W)), _const_spec((D, 2 * H)),
                  _const_spec((2 * H, D)), _const_spec((DN_CONV, 3 * W)),
                  _const_spec((1, H)), _const_spec((1, H)), _const_spec((H, 1)), _const_spec((H, 1)),
                  _const_spec((bw, 2)), _const_spec((4, bw))],
        out_specs=[pl.BlockSpec((1, tm, W), tok), pl.BlockSpec((1, tm, W), tok),
                   pl.BlockSpec((1, tm, W), tok), pl.BlockSpec((1, tm, 2 * H), tok),
                   pl.BlockSpec((1, tm // C, H, C), lambda b, j: (b, j, 0, 0))],
        out_shape=[jax.ShapeDtypeStruct((B, S, W), F32), jax.ShapeDtypeStruct((B, S, W), F32),
                   jax.ShapeDtypeStruct((B, S, W), F32), jax.ShapeDtypeStruct((B, S, 2 * H), F32),
                   jax.ShapeDtypeStruct((B, S // C, H, C), F32)],
        scratch_shapes=[pltpu.VMEM((tm + 8, bw), F32) for _ in range(3 * W // bw)],
        compiler_params=_params(("arbitrary", "arbitrary")),
        name="dn_in",
    )(x, gpre, wqkv, wba, wbat, conv_w, alog.reshape(1, H), dtb.reshape(1, H),
      alog.reshape(H, 1), dtb.reshape(H, 1), seg, segt)


def _bmm(a, b):
    return lax.dot_general(a.astype(BF16), b.astype(BF16), (((2,), (1,)), ((0,), (0,))),
                           preferred_element_type=F32)


def _bmm_nt(a, b):
    return lax.dot_general(a.astype(BF16), b.astype(BF16), (((2,), (2,)), ((0,), (0,))),
                           preferred_element_type=F32)


def _bmm_tn(a, b):
    return lax.dot_general(a.astype(BF16), b.astype(BF16), (((1,), (1,)), ((0,), (0,))),
                           preferred_element_type=F32)


def _unit_lower_inverse(low, ri, ci):
    C = low.shape[-1]
    eye = jnp.where(ri == ci, 1.0, 0.0).astype(F32)
    t = eye - jnp.where((ri // 2) == (ci // 2), low, 0.0)
    b = 2
    while b < C:
        off = jnp.where(((ri // (2 * b)) == (ci // (2 * b))) & ((ri // b) != (ci // b)), low, 0.0)
        t = t - _bmm(t, _bmm(off, t))
        b *= 2
    return t


def _dn_delta_kernel(q_ref, k_ref, v_ref, gcol_ref, grow_ref, o_ref, s_ref, *, tt):
    H, Dh, C = DN_HEADS, DN_HEAD_DIM, DN_CHUNK
    nc = tt // C

    @pl.when(pl.program_id(1) == 0)
    def _():
        s_ref[...] = jnp.zeros_like(s_ref)

    ri = lax.broadcasted_iota(jnp.int32, (1, C, C), 1)
    ci = lax.broadcasted_iota(jnp.int32, (1, C, C), 2)
    causal = ci <= ri
    strict = ci < ri

    def per_head(ref):
        return jnp.stack([ref[0, c * C:(c + 1) * C, hd * Dh:(hd + 1) * Dh]
                          for c in range(nc) for hd in range(H)])
    q, k, v = per_head(q_ref), per_head(k_ref), per_head(v_ref)
    gcol = gcol_ref[0]
    beta = jnp.stack([gcol[c * C:(c + 1) * C, hd:hd + 1] for c in range(nc) for hd in range(H)])
    gc = jnp.stack([gcol[c * C:(c + 1) * C, H + hd:H + hd + 1]
                    for c in range(nc) for hd in range(H)])
    gr = jnp.stack([grow_ref[0, c, hd:hd + 1, :] for c in range(nc) for hd in range(H)])
    gl = gc[:, C - 1:C, :]
    decay = jnp.where(causal, jnp.exp(jnp.where(causal, gc - gr, 0.0)), 0.0)
    egc = jnp.exp(gc)
    kb = k * beta
    kq = _bmm_nt(jnp.concatenate([kb, q], axis=1), k)
    low = jnp.where(strict, kq[:, :C] * decay, 0.0)
    attn = kq[:, C:] * decay
    t = _unit_lower_inverse(low, ri, ci)
    uw = _bmm(t, jnp.concatenate([v * beta, kb * egc], axis=2))
    u = uw[:, :, :Dh]
    wq = jnp.concatenate([uw[:, :, Dh:], q * egc], axis=1).astype(BF16)
    kdt = jnp.swapaxes(k * jnp.exp(gl - gc), 1, 2)
    akd = jnp.concatenate([attn, kdt], axis=1).astype(BF16)
    egl = jnp.exp(gl)

    state = s_ref[...]
    for c in range(nc):
        sl = slice(c * H, (c + 1) * H)
        ws = _bmm(wq[sl], state)
        v_new = u[sl] - ws[:, :C]
        av = _bmm(akd[sl], v_new)
        o = ws[:, C:] + av[:, :C]
        for hd in range(H):
            o_ref[0, c * C:(c + 1) * C, hd * Dh:(hd + 1) * Dh] = o[hd]
        state = state * egl[sl] + av[:, C:]
    s_ref[...] = state


def _dn_delta(q, k, v, gcol, grow, tt=256):
    B, S, W = q.shape
    H, Dh, C = DN_HEADS, DN_HEAD_DIM, DN_CHUNK
    tok = lambda b, j: (b, j, 0)
    return pl.pallas_call(
        functools.partial(_dn_delta_kernel, tt=tt),
        grid=(B, S // tt),
        in_specs=[pl.BlockSpec((1, tt, W), tok), pl.BlockSpec((1, tt, W), tok),
                  pl.BlockSpec((1, tt, W), tok), pl.BlockSpec((1, tt, 2 * H), tok),
                  pl.BlockSpec((1, tt // C, H, C), lambda b, j: (b, j, 0, 0))],
        out_specs=pl.BlockSpec((1, tt, W), tok),
        out_shape=jax.ShapeDtypeStruct((B, S, W), F32),
        scratch_shapes=[pltpu.VMEM((H, Dh, Dh), F32)],
        compiler_params=_params(("arbitrary", "arbitrary")),
        name="dn_delta",
    )(q, k, v, gcol, grow)


def _dn_out_kernel(x_ref, o_ref, gpre_ref, wz_ref, ng_ref, wout_ref, gpost_ref, y_ref):
    H, Dh = DN_HEADS, DN_HEAD_DIM
    x = x_ref[...]
    h = _rms(x, gpre_ref[...]).astype(BF16)
    z = jnp.dot(h, wz_ref[...], preferred_element_type=F32)
    ng = ng_ref[...]
    parts = []
    for hd in range(H):
        oh = o_ref[:, hd * Dh:(hd + 1) * Dh]
        parts.append((_rms(oh, ng) * _silu(z[:, hd * Dh:(hd + 1) * Dh])).astype(BF16))
    gated = jnp.concatenate(parts, axis=-1)
    m = jnp.dot(gated, wout_ref[...], preferred_element_type=F32)
    y_ref[...] = x + _rms(m, gpost_ref[...])


def _dn_out(x2, o2, gpre, wz, ng, wout, gpost, tm=512):
    T, D = x2.shape
    W = o2.shape[1]
    row = lambda i: (i, 0)
    return pl.pallas_call(
        _dn_out_kernel,
        grid=(T // tm,),
        in_specs=[pl.BlockSpec((tm, D), row), pl.BlockSpec((tm, W), row),
                  _const_spec((1, D)), _const_spec((D, W)), _const_spec((1, DN_HEAD_DIM)),
                  _const_spec((W, D)), _const_spec((1, D))],
        out_specs=pl.BlockSpec((tm, D), row),
        out_shape=jax.ShapeDtypeStruct((T, D), F32),
        compiler_params=_params(("parallel",)),
        name="dn_out",
    )(x2, o2, gpre, wz, ng, wout, gpost)


def _sg_kernel(x_ref, gpre_ref, win_ref, bin_ref, lng_ref, lnb_ref, ws_ref, bst_ref, wout_ref,
               gpost_ref, y_ref, *, tm):
    G, C = SG_GROUPS, SG_CHUNK
    E = win_ref.shape[1] // 2
    Eg = E // G
    x = x_ref[...]
    h = _rms(x, gpre_ref[...]).astype(BF16)

    def in_proj(col0):
        cols = slice(col0, col0 + Eg)
        z = jnp.dot(h, win_ref[:, cols], preferred_element_type=F32) + bin_ref[:, cols]
        return 0.5 * z * (1.0 + lax.erf(z * (2.0 ** -0.5)))

    vs = [in_proj(E + g * Eg) for g in range(G)]
    us = [in_proj(g * Eg) for g in range(G)]
    tot = vs[0]
    for g in range(1, G):
        tot = tot + vs[g]
    mu = jnp.sum(tot, axis=-1, keepdims=True) * (1.0 / E)
    xcs = [v - mu for v in vs]
    tot = xcs[0] * xcs[0]
    for g in range(1, G):
        tot = tot + xcs[g] * xcs[g]
    rstd = lax.rsqrt(jnp.sum(tot, axis=-1, keepdims=True) * (1.0 / E) + LN_EPS)

    ri = lax.broadcasted_iota(jnp.int32, (C, C), 0)
    ci = lax.broadcasted_iota(jnp.int32, (C, C), 1)
    bst = bst_ref[...]
    parts = []
    for g in range(G):
        cols = slice(g * Eg, (g + 1) * Eg)
        vn = (xcs[g] * rstd * lng_ref[:, cols] + lnb_ref[:, cols]).astype(BF16)
        wc = jnp.where(ci <= ri, ws_ref[g], 0.0).astype(BF16)
        rows = []
        for c in range(tm // C):
            mixed = jnp.dot(wc, vn[c * C:(c + 1) * C], preferred_element_type=F32) + bst[:, g:g + 1]
            rows.append((us[g][c * C:(c + 1) * C] * mixed).astype(BF16))
        parts.append(jnp.concatenate(rows, axis=0))
    gated = jnp.concatenate(parts, axis=-1)
    m = jnp.dot(gated, wout_ref[...], preferred_element_type=F32)
    y_ref[...] = x + _rms(m, gpost_ref[...])


def _sg(x2, gpre, win, b_in, lng, lnb, ws, bst, wout, gpost, tm=512):
    T, D = x2.shape
    E2 = win.shape[1]
    E = E2 // 2
    G, C = SG_GROUPS, SG_CHUNK
    row = lambda i: (i, 0)
    return pl.pallas_call(
        functools.partial(_sg_kernel, tm=tm),
        grid=(T // tm,),
        in_specs=[pl.BlockSpec((tm, D), row), _const_spec((1, D)), _const_spec((D, E2)),
                  _const_spec((1, E2)), _const_spec((1, E)), _const_spec((1, E)),
                  _const_spec((G, C, C)), _const_spec((C, G)), _const_spec((E, D)),
                  _const_spec((1, D))],
        out_specs=pl.BlockSpec((tm, D), row),
        out_shape=jax.ShapeDtypeStruct((T, D), F32),
        compiler_params=_params(("parallel",)),
        name="sg",
    )(x2, gpre, win, b_in, lng, lnb, ws, bst, wout, gpost)


def kernel(x, norm_g, ffn_w_gate, ffn_w_up, ffn_w_down, dn_w_in, dn_conv_w, dn_a_log, dn_dt_bias,
           dn_norm_g, dn_w_out, sg_w_in, sg_b_in, sg_ln_g, sg_ln_b, sg_w_s, sg_b_s, sg_w_out):
    B, S, D = x.shape
    depth = norm_g.shape[0]
    H = DN_HEADS
    W = H * DN_HEAD_DIM
    wg = ffn_w_gate.astype(BF16)
    wu = ffn_w_up.astype(BF16)
    wd = ffn_w_down.astype(BF16)
    x2 = x.reshape(B * S, D)
    for i in range(depth):
        ng = norm_g[i].reshape(6, 1, D)
        x2 = _ffn(x2, ng[0], wg[i, 0], wu[i, 0], wd[i, 0], ng[1])
        j = i // 2
        if i % 2 == 0:
            w_in = dn_w_in[j].astype(BF16)
            q, k, v, gcol, grow = _dn_in(x2.reshape(B, S, D), ng[2], w_in[:, :3 * W],
                                         w_in[:, 4 * W:], w_in[:, 4 * W:].T, dn_conv_w[j],
                                         dn_a_log[j], dn_dt_bias[j])
            o = _dn_delta(q, k, v, gcol, grow)
            x2 = _dn_out(x2, o.reshape(B * S, W), ng[2], w_in[:, 3 * W:4 * W],
                         dn_norm_g[j].reshape(1, -1), dn_w_out[j].astype(BF16), ng[3])
        else:
            E = sg_w_in.shape[2] // 2
            x2 = _sg(x2, ng[2], sg_w_in[j].astype(BF16), sg_b_in[j].reshape(1, -1),
                     sg_ln_g[j].reshape(1, E), sg_ln_b[j].reshape(1, E), sg_w_s[j], sg_b_s[j].T,
                     sg_w_out[j].astype(BF16), ng[3])
        x2 = _ffn(x2, ng[4], wg[i, 1], wu[i, 1], wd[i, 1], ng[5])
    return x2.reshape(B, S, D)
```

```python
import functools
import math

import jax
import jax.numpy as jnp
from jax import lax
from jax.experimental import pallas as pl
from jax.experimental.pallas import tpu as pltpu

RMS_EPS = 1e-6
LN_EPS = 1e-5
L2_EPS = 1e-6
DN_HEADS = 8
DN_HEAD_DIM = 128
DN_CONV = 4
DN_CHUNK = 64
SUBLANES = 8
DN_GROUP_CHUNKS = 4
SG_GROUPS = 8
SG_CHUNK = 128

VMEM_LIMIT_BYTES = 56 * 1024 * 1024

BF16 = jnp.bfloat16
F32 = jnp.float32


def _mm(a, b):
    return jnp.dot(a.astype(BF16), b.astype(BF16), preferred_element_type=F32)


def _mm_nt(a, b):
    return lax.dot_general(a.astype(BF16), b.astype(BF16), (((1,), (1,)), ((), ())),
                           preferred_element_type=F32)


def _mm_tn(a, b):
    return lax.dot_general(a.astype(BF16), b.astype(BF16), (((0,), (0,)), ((), ())),
                           preferred_element_type=F32)


def _rms(x, g):
    return x * lax.rsqrt(jnp.mean(x * x, axis=-1, keepdims=True) + RMS_EPS) * g


def _silu(x):
    return x * jax.nn.sigmoid(x)


def _const_spec(shape):
    nd = len(shape)
    return pl.BlockSpec(shape, lambda *_: (0,) * nd, pipeline_mode=pl.Buffered(1))


def _params(sem):
    return pltpu.CompilerParams(dimension_semantics=sem, vmem_limit_bytes=VMEM_LIMIT_BYTES)


def _ffn_kernel(x_ref, gpre_ref, wg_ref, wu_ref, wd_ref, gpost_ref, o_ref, *, halves):
    rows = x_ref.shape[0] // halves
    sls = [slice(i * rows, (i + 1) * rows) for i in range(halves)]
    hs = [_rms(x_ref[sl, :], gpre_ref[...]).astype(BF16) for sl in sls]
    ys = []
    for i, sl in enumerate(sls):
        gate = jnp.dot(hs[i], wg_ref[...], preferred_element_type=F32)
        up = jnp.dot(hs[i], wu_ref[...], preferred_element_type=F32)
        a = (_silu(gate) * up).astype(BF16)
        ys.append(jnp.dot(a, wd_ref[...], preferred_element_type=F32))
        if i > 0:
            o_ref[sls[i - 1], :] = x_ref[sls[i - 1], :] + 0.5 * _rms(ys[i - 1], gpost_ref[...])
    o_ref[sls[-1], :] = x_ref[sls[-1], :] + 0.5 * _rms(ys[-1], gpost_ref[...])


def _ffn(x2, gpre, wg, wu, wd, gpost, tm=1024, halves=2):
    T, D = x2.shape
    F = wg.shape[1]
    return pl.pallas_call(
        functools.partial(_ffn_kernel, halves=halves),
        grid=(T // tm,),
        in_specs=[pl.BlockSpec((tm, D), lambda i: (i, 0)),
                  _const_spec((1, D)), _const_spec((D, F)), _const_spec((D, F)),
                  _const_spec((F, D)), _const_spec((1, D))],
        out_specs=pl.BlockSpec((tm, D), lambda i: (i, 0)),
        out_shape=jax.ShapeDtypeStruct((T, D), F32),
        compiler_params=_params(("parallel",)),
        name="ffn",
    )(x2, gpre, wg, wu, wd, gpost)


def _dn_in_kernel(x_ref, gpre_ref, wqkv_ref, wba_ref, wbat_ref, conv_ref, alog_c_ref, dtb_c_ref,
                  alog_r_ref, dtb_r_ref, seg_ref, segt_ref,
                  q_ref, k_ref, v_ref, gcol_ref, grow_ref, *ext_refs, tm):
    H, Dh, C = DN_HEADS, DN_HEAD_DIM, DN_CHUNK
    bw = 2 * Dh
    nb = H * Dh // bw

    @pl.when(pl.program_id(1) == 0)
    def _():
        for ext in ext_refs:
            ext[0:8, :] = jnp.zeros((8, bw), F32)

    x = x_ref[0]
    h = _rms(x, gpre_ref[...]).astype(BF16)
    n_blocks = len(ext_refs)

    def project(cb):
        cols = slice(cb * bw, (cb + 1) * bw)
        proj = jnp.dot(h, wqkv_ref[:, cols], preferred_element_type=F32)
        ext_refs[cb][8:8 + tm, :] = proj
        return proj

    projs = {0: project(0), 1: project(1)}
    for cb, ext in enumerate(ext_refs):
        cols = slice(cb * bw, (cb + 1) * bw)
        proj = projs.pop(cb)
        cw = conv_ref[:, cols]
        acc = proj * cw[DN_CONV - 1:DN_CONV, :]
        for s in range(1, DN_CONV):
            acc = acc + ext[8 - s:8 - s + tm, :] * cw[DN_CONV - 1 - s:DN_CONV - s, :]
        ext[0:8, :] = ext[tm:tm + 8, :]
        y = _silu(acc)
        normed = cb < 2 * nb
        if normed:
            ss = jnp.dot((y * y).astype(BF16), seg_ref[...], preferred_element_type=F32)
        if cb + 2 < n_blocks:
            projs[cb + 2] = project(cb + 2)
        if normed:
            inv = lax.rsqrt(ss + L2_EPS) * (Dh ** -0.5 if cb < nb else 1.0)
            hi = inv.astype(BF16)
            lo = (inv - hi.astype(F32)).astype(BF16)
            y = y * jnp.dot(jnp.concatenate([hi, lo], axis=1), segt_ref[...],
                            preferred_element_type=F32)
        out_ref = (q_ref, k_ref, v_ref)[cb // nb]
        out_ref[0, :, (cb % nb) * bw:(cb % nb + 1) * bw] = y

    ba = jnp.dot(h, wba_ref[...], preferred_element_type=F32)
    bat = lax.dot_general(wbat_ref[...], h, (((1,), (1,)), ((), ())),
                          preferred_element_type=F32)
    beta_c = jax.nn.sigmoid(ba[:, :H])
    g_c = -jnp.exp(alog_c_ref[...]) * jax.nn.softplus(ba[:, H:] + dtb_c_ref[...])
    g_r = -jnp.exp(alog_r_ref[...]) * jax.nn.softplus(bat[H:, :] + dtb_r_ref[...])
    ri = lax.broadcasted_iota(jnp.int32, (tm, tm), 0)
    ci = lax.broadcasted_iota(jnp.int32, (tm, tm), 1)
    same = (ri // C) == (ci // C)
    tril = jnp.where(same & (ci <= ri), 1.0, 0.0).astype(F32)
    triu = jnp.where(same & (ri <= ci), 1.0, 0.0).astype(F32)
    gc_c = jnp.dot(tril, g_c, precision=lax.Precision.HIGHEST, preferred_element_type=F32)
    gc_r = jnp.dot(g_r, triu, precision=lax.Precision.HIGHEST, preferred_element_type=F32)
    gcol_ref[0, :, 0:H] = beta_c
    gcol_ref[0, :, H:2 * H] = gc_c
    for c in range(tm // C):
        grow_ref[0, c] = gc_r[:, c * C:(c + 1) * C]


def _dn_in(x, gpre, wqkv, wba, wbat, conv_w, alog, dtb, tm=256):
    B, S, D = x.shape
    H, C = DN_HEADS, DN_CHUNK
    W = H * DN_HEAD_DIM
    tok = lambda b, j: (b, j, 0)
    bw = 2 * DN_HEAD_DIM
    seg = (jnp.arange(bw)[:, None] // DN_HEAD_DIM == jnp.arange(2)[None, :]).astype(BF16)
    segt = jnp.concatenate([seg.T, seg.T], axis=0)
    return pl.pallas_call(
        functools.partial(_dn_in_kernel, tm=tm),
        grid=(B, S // tm),
        in_specs=[pl.BlockSpec((1, tm, D), tok),
                  _const_spec((1, D)), _const_spec((D, 3 * W)), _const_spec((D, 2 * H)),
                  _const_spec((2 * H, D)), _const_spec((DN_CONV, 3 * W)),
                  _const_spec((1, H)), _const_spec((1, H)), _const_spec((H, 1)), _const_spec((H, 1)),
                  _const_spec((bw, 2)), _const_spec((4, bw))],
        out_specs=[pl.BlockSpec((1, tm, W), tok), pl.BlockSpec((1, tm, W), tok),
                   pl.BlockSpec((1, tm, W), tok), pl.BlockSpec((1, tm, 2 * H), tok),
                   pl.BlockSpec((1, tm // C, H, C), lambda b, j: (b, j, 0, 0))],
        out_shape=[jax.ShapeDtypeStruct((B, S, W), F32), jax.ShapeDtypeStruct((B, S, W), F32),
                   jax.ShapeDtypeStruct((B, S, W), F32), jax.ShapeDtypeStruct((B, S, 2 * H), F32),
                   jax.ShapeDtypeStruct((B, S // C, H, C), F32)],
        scratch_shapes=[pltpu.VMEM((tm + 8, bw), F32) for _ in range(3 * W // bw)],
        compiler_params=_params(("arbitrary", "arbitrary")),
        name="dn_in",
    )(x, gpre, wqkv, wba, wbat, conv_w, alog.reshape(1, H), dtb.reshape(1, H),
      alog.reshape(H, 1), dtb.reshape(H, 1), seg, segt)


def _bmm(a, b):
    return lax.dot_general(a.astype(BF16), b.astype(BF16), (((2,), (1,)), ((0,), (0,))),
                           preferred_element_type=F32)


def _bmm_nt(a, b):
    return lax.dot_general(a.astype(BF16), b.astype(BF16), (((2,), (2,)), ((0,), (0,))),
                           preferred_element_type=F32)


def _bmm_tn(a, b):
    return lax.dot_general(a.astype(BF16), b.astype(BF16), (((1,), (1,)), ((0,), (0,))),
                           preferred_element_type=F32)


def _unit_lower_inverse_steps(low, ri, ci):
    C = low.shape[-1]
    eye = jnp.where(ri == ci, 1.0, 0.0).astype(F32)
    t = eye - jnp.where((ri // 2) == (ci // 2), low, 0.0)
    b = 2
    while b < C:
        off = jnp.where(((ri // (2 * b)) == (ci // (2 * b))) & ((ri // b) != (ci // b)), low, 0.0)
        if b < SUBLANES:
            t = t - _bmm(t, _bmm(off, t))
        else:
            lower = [slice(s, s + b) for s in range(b, C, 2 * b)]
            rows = lambda a: jnp.concatenate([a[:, r] for r in lower], axis=1)
            x_lo = _bmm(rows(off), t)
            zero = jnp.zeros_like(x_lo[:, :b])
            x = jnp.concatenate([p for i in range(len(lower))
                                 for p in (zero, x_lo[:, i * b:(i + 1) * b])], axis=1)
            corr = _bmm(rows(t), x)
            t = jnp.concatenate([p for i, r in enumerate(lower)
                                 for p in (t[:, r.start - b:r.start],
                                           t[:, r] - corr[:, i * b:(i + 1) * b])], axis=1)
        b *= 2
        yield None
    yield t


def _dn_delta_kernel(q_ref, k_ref, v_ref, gcol_ref, grow_ref, spread_ref, o_ref, s_ref, *, tt):
    H, Dh, C = DN_HEADS, DN_HEAD_DIM, DN_CHUNK
    nc = tt // C

    @pl.when(pl.program_id(1) == 0)
    def _():
        s_ref[...] = jnp.zeros_like(s_ref)

    ri = lax.broadcasted_iota(jnp.int32, (1, C, C), 1)
    ci = lax.broadcasted_iota(jnp.int32, (1, C, C), 2)
    causal = ci <= ri
    strict = ci < ri

    gates = gcol_ref[0]
    g_hi = gates.astype(BF16)
    g_r1 = gates - g_hi.astype(F32)
    g_mid = g_r1.astype(BF16)
    g_lo = (g_r1 - g_mid.astype(F32)).astype(BF16)
    spread = jnp.dot(jnp.concatenate([g_hi, g_mid, g_lo], axis=1), spread_ref[...],
                     preferred_element_type=F32)

    def local_steps(chunks):
        per_head = lambda ref: jnp.stack([ref[0, c * C:(c + 1) * C, hd * Dh:(hd + 1) * Dh]
                                          for c in chunks for hd in range(H)])
        q, k, v = per_head(q_ref), per_head(k_ref), per_head(v_ref)
        beta = jnp.stack([spread[c * C:(c + 1) * C, hd * Dh:(hd + 1) * Dh]
                          for c in chunks for hd in range(H)])
        gc = jnp.stack([spread[c * C:(c + 1) * C, (H + hd) * Dh:(H + hd + 1) * Dh]
                        for c in chunks for hd in range(H)])
        gr = jnp.stack([grow_ref[0, c, hd:hd + 1, :] for c in chunks for hd in range(H)])
        gl = gc[:, C - 1:C, :]
        decay = jnp.where(causal, jnp.exp(jnp.where(causal, gc[:, :, :C] - gr, 0.0)), 0.0)
        egc = jnp.exp(gc)
        kb = k * beta
        kq = _bmm_nt(jnp.concatenate([kb, q], axis=1), k)
        yield None
        low = jnp.where(strict, kq[:, :C] * decay, 0.0)
        attn = kq[:, C:] * decay
        t = None
        for t in _unit_lower_inverse_steps(low, ri, ci):
            if t is None:
                yield None
        uw = _bmm(t, jnp.concatenate([v * beta, kb * egc], axis=2))
        wq = jnp.concatenate([uw[:, :, Dh:], q * egc], axis=1).astype(BF16)
        kdt = jnp.swapaxes(k * jnp.exp(gl - gc), 1, 2)
        akd = jnp.concatenate([attn, kdt], axis=1).astype(BF16)
        yield dict(u=uw[:, :, :Dh], wq=wq, akd=akd, egl=jnp.exp(gl))

    def recurrent_steps(chunks, loc, state):
        for i, c in enumerate(chunks):
            sl = slice(i * H, (i + 1) * H)
            ws = _bmm(loc["wq"][sl], state)
            v_new = loc["u"][sl] - ws[:, :C]
            yield None
            av = _bmm(loc["akd"][sl], v_new)
            o = ws[:, C:] + av[:, :C]
            for hd in range(H):
                o_ref[0, c * C:(c + 1) * C, hd * Dh:(hd + 1) * Dh] = o[hd]
            state = state * loc["egl"][sl] + av[:, C:]
            yield None
        yield state

    groups = [list(range(g, min(g + DN_GROUP_CHUNKS, nc))) for g in range(0, nc, DN_GROUP_CHUNKS)]
    state = s_ref[...]
    loc_prev = None
    for gi in range(len(groups) + 1):
        rec = recurrent_steps(groups[gi - 1], loc_prev, state) if gi > 0 else iter(())
        loc = None
        if gi < len(groups):
            for loc in local_steps(groups[gi]):
                out = next(rec, None)
                if out is not None:
                    state = out
        for out in rec:
            if out is not None:
                state = out
        loc_prev = loc
    s_ref[...] = state


def _dn_delta(q, k, v, gcol, grow, tt=512):
    B, S, W = q.shape
    H, Dh, C = DN_HEADS, DN_HEAD_DIM, DN_CHUNK
    tok = lambda b, j: (b, j, 0)
    spread = (jnp.arange(2 * H)[:, None] == jnp.arange(2 * W)[None, :] // Dh).astype(BF16)
    spread = jnp.concatenate([spread] * 3, axis=0)
    return pl.pallas_call(
        functools.partial(_dn_delta_kernel, tt=tt),
        grid=(B, S // tt),
        in_specs=[pl.BlockSpec((1, tt, W), tok), pl.BlockSpec((1, tt, W), tok),
                  pl.BlockSpec((1, tt, W), tok), pl.BlockSpec((1, tt, 2 * H), tok),
                  pl.BlockSpec((1, tt // C, H, C), lambda b, j: (b, j, 0, 0)),
                  _const_spec((3 * 2 * H, 2 * W))],
        out_specs=pl.BlockSpec((1, tt, W), tok),
        out_shape=jax.ShapeDtypeStruct((B, S, W), F32),
        scratch_shapes=[pltpu.VMEM((H, Dh, Dh), F32)],
        compiler_params=_params(("arbitrary", "arbitrary")),
        name="dn_delta",
    )(q, k, v, gcol, grow, spread)


def _dn_out_kernel(x_ref, o_ref, gpre_ref, wz_ref, ng_ref, wout_ref, gpost_ref, y_ref):
    H, Dh = DN_HEADS, DN_HEAD_DIM
    x = x_ref[...]
    h = _rms(x, gpre_ref[...]).astype(BF16)
    z = jnp.dot(h, wz_ref[...], preferred_element_type=F32)
    ng = ng_ref[...]
    parts = []
    for hd in range(H):
        oh = o_ref[:, hd * Dh:(hd + 1) * Dh]
        parts.append((_rms(oh, ng) * _silu(z[:, hd * Dh:(hd + 1) * Dh])).astype(BF16))
    gated = jnp.concatenate(parts, axis=-1)
    m = jnp.dot(gated, wout_ref[...], preferred_element_type=F32)
    y_ref[...] = x + _rms(m, gpost_ref[...])


def _dn_out(x2, o2, gpre, wz, ng, wout, gpost, tm=512):
    T, D = x2.shape
    W = o2.shape[1]
    row = lambda i: (i, 0)
    return pl.pallas_call(
        _dn_out_kernel,
        grid=(T // tm,),
        in_specs=[pl.BlockSpec((tm, D), row), pl.BlockSpec((tm, W), row),
                  _const_spec((1, D)), _const_spec((D, W)), _const_spec((1, DN_HEAD_DIM)),
                  _const_spec((W, D)), _const_spec((1, D))],
        out_specs=pl.BlockSpec((tm, D), row),
        out_shape=jax.ShapeDtypeStruct((T, D), F32),
        compiler_params=_params(("parallel",)),
        name="dn_out",
    )(x2, o2, gpre, wz, ng, wout, gpost)


def _sg_kernel(x_ref, gpre_ref, win_ref, bin_ref, lng_ref, lnb_ref, ws_ref, bst_ref, wout_ref,
               gpost_ref, y_ref, *, tm):
    G, C = SG_GROUPS, SG_CHUNK
    E = win_ref.shape[1] // 2
    Eg = E // G
    x = x_ref[...]
    h = _rms(x, gpre_ref[...]).astype(BF16)

    def in_proj(col0):
        cols = slice(col0, col0 + Eg)
        z = jnp.dot(h, win_ref[:, cols], preferred_element_type=F32) + bin_ref[:, cols]
        return 0.5 * z * (1.0 + lax.erf(z * (2.0 ** -0.5)))

    vs = [in_proj(E + g * Eg) for g in range(G)]
    us = [in_proj(g * Eg) for g in range(G)]
    tot = vs[0]
    for g in range(1, G):
        tot = tot + vs[g]
    mu = jnp.sum(tot, axis=-1, keepdims=True) * (1.0 / E)
    xcs = [v - mu for v in vs]
    tot = xcs[0] * xcs[0]
    for g in range(1, G):
        tot = tot + xcs[g] * xcs[g]
    rstd = lax.rsqrt(jnp.sum(tot, axis=-1, keepdims=True) * (1.0 / E) + LN_EPS)

    ri = lax.broadcasted_iota(jnp.int32, (C, C), 0)
    ci = lax.broadcasted_iota(jnp.int32, (C, C), 1)
    bst = bst_ref[...]
    parts = []
    for g in range(G):
        cols = slice(g * Eg, (g + 1) * Eg)
        vn = (xcs[g] * rstd * lng_ref[:, cols] + lnb_ref[:, cols]).astype(BF16)
        wc = jnp.where(ci <= ri, ws_ref[g], 0.0).astype(BF16)
        rows = []
        for c in range(tm // C):
            mixed = jnp.dot(wc, vn[c * C:(c + 1) * C], preferred_element_type=F32) + bst[:, g:g + 1]
            rows.append((us[g][c * C:(c + 1) * C] * mixed).astype(BF16))
        parts.append(jnp.concatenate(rows, axis=0))
    gated = jnp.concatenate(parts, axis=-1)
    m = jnp.dot(gated, wout_ref[...], preferred_element_type=F32)
    y_ref[...] = x + _rms(m, gpost_ref[...])


def _sg(x2, gpre, win, b_in, lng, lnb, ws, bst, wout, gpost, tm=512):
    T, D = x2.shape
    E2 = win.shape[1]
    E = E2 // 2
    G, C = SG_GROUPS, SG_CHUNK
    row = lambda i: (i, 0)
    return pl.pallas_call(
        functools.partial(_sg_kernel, tm=tm),
        grid=(T // tm,),
        in_specs=[pl.BlockSpec((tm, D), row), _const_spec((1, D)), _const_spec((D, E2)),
                  _const_spec((1, E2)), _const_spec((1, E)), _const_spec((1, E)),
                  _const_spec((G, C, C)), _const_spec((C, G)), _const_spec((E, D)),
                  _const_spec((1, D))],
        out_specs=pl.BlockSpec((tm, D), row),
        out_shape=jax.ShapeDtypeStruct((T, D), F32),
        compiler_params=_params(("parallel",)),
        name="sg",
    )(x2, gpre, win, b_in, lng, lnb, ws, bst, wout, gpost)


def kernel(x, norm_g, ffn_w_gate, ffn_w_up, ffn_w_down, dn_w_in, dn_conv_w, dn_a_log, dn_dt_bias,
           dn_norm_g, dn_w_out, sg_w_in, sg_b_in, sg_ln_g, sg_ln_b, sg_w_s, sg_b_s, sg_w_out):
    B, S, D = x.shape
    depth = norm_g.shape[0]
    H = DN_HEADS
    W = H * DN_HEAD_DIM
    wg = ffn_w_gate.astype(BF16)
    wu = ffn_w_up.astype(BF16)
    wd = ffn_w_down.astype(BF16)
    x2 = x.reshape(B * S, D)
    for i in range(depth):
        ng = norm_g[i].reshape(6, 1, D)
        x2 = _ffn(x2, ng[0], wg[i, 0], wu[i, 0], wd[i, 0], ng[1])
        j = i // 2
        if i % 2 == 0:
            w_in = dn_w_in[j].astype(BF16)
            q, k, v, gcol, grow = _dn_in(x2.reshape(B, S, D), ng[2], w_in[:, :3 * W],
                                         w_in[:, 4 * W:], w_in[:, 4 * W:].T, dn_conv_w[j],
                                         dn_a_log[j], dn_dt_bias[j])
            o = _dn_delta(q, k, v, gcol, grow)
            x2 = _dn_out(x2, o.reshape(B * S, W), ng[2], w_in[:, 3 * W:4 * W],
                         dn_norm_g[j].reshape(1, -1), dn_w_out[j].astype(BF16), ng[3])
        else:
            E = sg_w_in.shape[2] // 2
            x2 = _sg(x2, ng[2], sg_w_in[j].astype(BF16), sg_b_in[j].reshape(1, -1),
                     sg_ln_g[j].reshape(1, E), sg_ln_b[j].reshape(1, E), sg_w_s[j], sg_b_s[j].T,
                     sg_w_out[j].astype(BF16), ng[3])
        x2 = _ffn(x2, ng[4], wg[i, 1], wu[i, 1], wd[i, 1], ng[5])
    return x2.reshape(B, S, D)
```

```python
import functools
import math

import jax
import jax.numpy as jnp
from jax import lax
from jax.experimental import pallas as pl
from jax.experimental.pallas import tpu as pltpu

RMS_EPS = 1e-6
LN_EPS = 1e-5
L2_EPS = 1e-6
DN_HEADS = 8
DN_HEAD_DIM = 128
DN_CONV = 4
DN_CHUNK = 64
SUBLANES = 8
DN_GROUP_CHUNKS = 4
SG_GROUPS = 8
SG_CHUNK = 128

VMEM_LIMIT_BYTES = 56 * 1024 * 1024

BF16 = jnp.bfloat16
F32 = jnp.float32


def _mm(a, b):
    return jnp.dot(a.astype(BF16), b.astype(BF16), preferred_element_type=F32)


def _mm_nt(a, b):
    return lax.dot_general(a.astype(BF16), b.astype(BF16), (((1,), (1,)), ((), ())),
                           preferred_element_type=F32)


def _mm_tn(a, b):
    return lax.dot_general(a.astype(BF16), b.astype(BF16), (((0,), (0,)), ((), ())),
                           preferred_element_type=F32)


def _bf16_terms(x):
    hi = x.astype(BF16)
    r1 = x - hi.astype(F32)
    mid = r1.astype(BF16)
    lo = (r1 - mid.astype(F32)).astype(BF16)
    return [hi, mid, lo]


def _rms(x, g):
    return x * lax.rsqrt(jnp.mean(x * x, axis=-1, keepdims=True) + RMS_EPS) * g


def _silu(x):
    return x * jax.nn.sigmoid(x)


def _const_spec(shape):
    nd = len(shape)
    return pl.BlockSpec(shape, lambda *_: (0,) * nd, pipeline_mode=pl.Buffered(1))


def _col_block_spec(rows, width, block):
    return pl.BlockSpec((rows, width), lambda *_: (0, block), pipeline_mode=pl.Buffered(1))


def _params(sem):
    return pltpu.CompilerParams(dimension_semantics=sem, vmem_limit_bytes=VMEM_LIMIT_BYTES)


def _ffn_kernel(x_ref, gpre_ref, wg_ref, wu_ref, wd_ref, gpost_ref, o_ref, *, halves):
    rows = x_ref.shape[0] // halves
    sls = [slice(i * rows, (i + 1) * rows) for i in range(halves)]
    hs = [_rms(x_ref[sl, :], gpre_ref[...]).astype(BF16) for sl in sls]
    ys = []
    for i, sl in enumerate(sls):
        gate = jnp.dot(hs[i], wg_ref[...], preferred_element_type=F32)
        up = jnp.dot(hs[i], wu_ref[...], preferred_element_type=F32)
        a = (_silu(gate) * up).astype(BF16)
        ys.append(jnp.dot(a, wd_ref[...], preferred_element_type=F32))
        if i > 0:
            o_ref[sls[i - 1], :] = x_ref[sls[i - 1], :] + 0.5 * _rms(ys[i - 1], gpost_ref[...])
    o_ref[sls[-1], :] = x_ref[sls[-1], :] + 0.5 * _rms(ys[-1], gpost_ref[...])


def _ffn(x2, gpre, wg, wu, wd, gpost, tm=1024, halves=2):
    T, D = x2.shape
    F = wg.shape[1]
    return pl.pallas_call(
        functools.partial(_ffn_kernel, halves=halves),
        grid=(T // tm,),
        in_specs=[pl.BlockSpec((tm, D), lambda i: (i, 0)),
                  _const_spec((1, D)), _const_spec((D, F)), _const_spec((D, F)),
                  _const_spec((F, D)), _const_spec((1, D))],
        out_specs=pl.BlockSpec((tm, D), lambda i: (i, 0)),
        out_shape=jax.ShapeDtypeStruct((T, D), F32),
        compiler_params=_params(("parallel",)),
        name="ffn",
    )(x2, gpre, wg, wu, wd, gpost)


def _dn_in_kernel(x_ref, gpre_ref, wqkv_ref, wba_ref, wbat_ref, conv_ref, alog_c_ref, dtb_c_ref,
                  alog_r_ref, dtb_r_ref,
                  q_ref, k_ref, v_ref, gcol_ref, grow_ref, *ext_refs, tm):
    H, Dh, C = DN_HEADS, DN_HEAD_DIM, DN_CHUNK
    bw = 2 * Dh
    nb = H * Dh // bw

    @pl.when(pl.program_id(1) == 0)
    def _():
        for ext in ext_refs:
            ext[0:8, :] = jnp.zeros((8, bw), F32)

    x = x_ref[0]
    h = _rms(x, gpre_ref[...]).astype(BF16)
    n_blocks = len(ext_refs)

    def project(cb):
        cols = slice(cb * bw, (cb + 1) * bw)
        proj = jnp.dot(h, wqkv_ref[:, cols], preferred_element_type=F32)
        ext_refs[cb][8:8 + tm, :] = proj
        return proj

    ba = jnp.dot(h, wba_ref[...], preferred_element_type=F32)
    bat = lax.dot_general(wbat_ref[...], h, (((1,), (1,)), ((), ())),
                          preferred_element_type=F32)
    projs = {0: project(0), 1: project(1)}
    beta_c = jax.nn.sigmoid(ba[:, :H])
    g_c = -jnp.exp(alog_c_ref[...]) * jax.nn.softplus(ba[:, H:] + dtb_c_ref[...])
    g_r = -jnp.exp(alog_r_ref[...]) * jax.nn.softplus(bat[H:, :] + dtb_r_ref[...])
    ri = lax.broadcasted_iota(jnp.int32, (tm, tm), 0)
    ci = lax.broadcasted_iota(jnp.int32, (tm, tm), 1)
    same = (ri // C) == (ci // C)
    tril = jnp.where(same & (ci <= ri), 1.0, 0.0).astype(BF16)
    triu = jnp.where(same & (ri <= ci), 1.0, 0.0).astype(BF16)
    gc_c = sum(jnp.dot(tril, term, preferred_element_type=F32) for term in _bf16_terms(g_c))
    gc_r3 = jnp.dot(jnp.concatenate(_bf16_terms(g_r), axis=0), triu, preferred_element_type=F32)
    gc_r = gc_r3[0:H] + gc_r3[H:2 * H] + gc_r3[2 * H:3 * H]
    gcol_ref[0, :, 0:H] = beta_c
    gcol_ref[0, :, H:2 * H] = gc_c
    for c in range(tm // C):
        grow_ref[0, c] = gc_r[:, c * C:(c + 1) * C]

    for cb, ext in enumerate(ext_refs):
        cols = slice(cb * bw, (cb + 1) * bw)
        proj = projs.pop(cb)
        cw = conv_ref[:, cols]
        acc = proj * cw[DN_CONV - 1:DN_CONV, :]
        for s in range(1, DN_CONV):
            acc = acc + ext[8 - s:8 - s + tm, :] * cw[DN_CONV - 1 - s:DN_CONV - s, :]
        ext[0:8, :] = ext[tm:tm + 8, :]
        y = _silu(acc)
        if cb + 2 < n_blocks:
            projs[cb + 2] = project(cb + 2)
        if cb < 2 * nb:
            scale = Dh ** -0.5 if cb < nb else 1.0
            heads = [y[:, i * Dh:(i + 1) * Dh] for i in range(bw // Dh)]
            y = jnp.concatenate(
                [yh * (lax.rsqrt(jnp.sum(yh * yh, axis=-1, keepdims=True) + L2_EPS) * scale)
                 for yh in heads], axis=1)
        out_ref = (q_ref, k_ref, v_ref)[cb // nb]
        out_ref[0, :, (cb % nb) * bw:(cb % nb + 1) * bw] = y


def _dn_in(x, gpre, wqkv, wba, wbat, conv_w, alog, dtb, tm=256):
    B, S, D = x.shape
    H, C = DN_HEADS, DN_CHUNK
    W = H * DN_HEAD_DIM
    tok = lambda b, j: (b, j, 0)
    bw = 2 * DN_HEAD_DIM
    return pl.pallas_call(
        functools.partial(_dn_in_kernel, tm=tm),
        grid=(B, S // tm),
        in_specs=[pl.BlockSpec((1, tm, D), tok),
                  _const_spec((1, D)), _col_block_spec(D, 3 * W, 0), _const_spec((D, 2 * H)),
                  _const_spec((2 * H, D)), _const_spec((DN_CONV, 3 * W)),
                  _const_spec((1, H)), _const_spec((1, H)), _const_spec((H, 1)), _const_spec((H, 1))],
        out_specs=[pl.BlockSpec((1, tm, W), tok), pl.BlockSpec((1, tm, W), tok),
                   pl.BlockSpec((1, tm, W), tok), pl.BlockSpec((1, tm, 2 * H), tok),
                   pl.BlockSpec((1, tm // C, H, C), lambda b, j: (b, j, 0, 0))],
        out_shape=[jax.ShapeDtypeStruct((B, S, W), F32), jax.ShapeDtypeStruct((B, S, W), F32),
                   jax.ShapeDtypeStruct((B, S, W), F32), jax.ShapeDtypeStruct((B, S, 2 * H), F32),
                   jax.ShapeDtypeStruct((B, S // C, H, C), F32)],
        scratch_shapes=[pltpu.VMEM((tm + 8, bw), F32) for _ in range(3 * W // bw)],
        compiler_params=_params(("arbitrary", "arbitrary")),
        name="dn_in",
    )(x, gpre, wqkv, wba, wbat, conv_w, alog.reshape(1, H), dtb.reshape(1, H),
      alog.reshape(H, 1), dtb.reshape(H, 1))


def _bmm(a, b):
    return lax.dot_general(a.astype(BF16), b.astype(BF16), (((2,), (1,)), ((0,), (0,))),
                           preferred_element_type=F32)


def _bmm_nt(a, b):
    return lax.dot_general(a.astype(BF16), b.astype(BF16), (((2,), (2,)), ((0,), (0,))),
                           preferred_element_type=F32)


def _bmm_tn(a, b):
    return lax.dot_general(a.astype(BF16), b.astype(BF16), (((1,), (1,)), ((0,), (0,))),
                           preferred_element_type=F32)


def _unit_lower_inverse_steps(low, ri, ci):
    C = low.shape[-1]
    eye = jnp.where(ri == ci, 1.0, 0.0).astype(F32)
    t = eye - jnp.where((ri // 2) == (ci // 2), low, 0.0)
    b = 2
    while b < C:
        off = jnp.where(((ri // (2 * b)) == (ci // (2 * b))) & ((ri // b) != (ci // b)), low, 0.0)
        if b < SUBLANES:
            t = t - _bmm(t, _bmm(off, t))
        else:
            lower = [slice(s, s + b) for s in range(b, C, 2 * b)]
            rows = lambda a: jnp.concatenate([a[:, r] for r in lower], axis=1)
            x_lo = _bmm(rows(off), t)
            zero = jnp.zeros_like(x_lo[:, :b])
            x = jnp.concatenate([p for i in range(len(lower))
                                 for p in (zero, x_lo[:, i * b:(i + 1) * b])], axis=1)
            corr = _bmm(rows(t), x)
            t = jnp.concatenate([p for i, r in enumerate(lower)
                                 for p in (t[:, r.start - b:r.start],
                                           t[:, r] - corr[:, i * b:(i + 1) * b])], axis=1)
        b *= 2
        yield None
    yield t


def _dn_delta_kernel(q_ref, k_ref, v_ref, gcol_ref, grow_ref, spread_ref, o_ref, s_ref, *, tt):
    H, Dh, C = DN_HEADS, DN_HEAD_DIM, DN_CHUNK
    nc = tt // C

    @pl.when(pl.program_id(1) == 0)
    def _():
        s_ref[...] = jnp.zeros_like(s_ref)

    ri = lax.broadcasted_iota(jnp.int32, (1, C, C), 1)
    ci = lax.broadcasted_iota(jnp.int32, (1, C, C), 2)
    causal = ci <= ri
    strict = ci < ri

    gates = gcol_ref[0]
    spread = jnp.dot(jnp.concatenate(_bf16_terms(gates), axis=1), spread_ref[...],
                     preferred_element_type=F32)

    def local_steps(chunks):
        per_head = lambda ref: jnp.stack([ref[0, c * C:(c + 1) * C, hd * Dh:(hd + 1) * Dh]
                                          for c in chunks for hd in range(H)])
        q, k, v = per_head(q_ref), per_head(k_ref), per_head(v_ref)
        beta = jnp.stack([spread[c * C:(c + 1) * C, hd * Dh:(hd + 1) * Dh]
                          for c in chunks for hd in range(H)])
        gc = jnp.stack([spread[c * C:(c + 1) * C, (H + hd) * Dh:(H + hd + 1) * Dh]
                        for c in chunks for hd in range(H)])
        gr = jnp.stack([grow_ref[0, c, hd:hd + 1, :] for c in chunks for hd in range(H)])
        gl = gc[:, C - 1:C, :]
        decay = jnp.where(causal, jnp.exp(jnp.where(causal, gc[:, :, :C] - gr, 0.0)), 0.0)
        egc = jnp.exp(gc)
        kb = k * beta
        kq = _bmm_nt(jnp.concatenate([kb, q], axis=1), k)
        yield None
        low = jnp.where(strict, kq[:, :C] * decay, 0.0)
        attn = kq[:, C:] * decay
        t = None
        for t in _unit_lower_inverse_steps(low, ri, ci):
            if t is None:
                yield None
        uw = _bmm(t, jnp.concatenate([v * beta, kb * egc], axis=2))
        wq = jnp.concatenate([uw[:, :, Dh:], q * egc], axis=1).astype(BF16)
        kdt = jnp.swapaxes(k * jnp.exp(gl - gc), 1, 2)
        akd = jnp.concatenate([attn, kdt], axis=1).astype(BF16)
        yield dict(u=uw[:, :, :Dh], wq=wq, akd=akd, egl=jnp.exp(gl))

    def recurrent_steps(chunks, loc, state):
        for i, c in enumerate(chunks):
            sl = slice(i * H, (i + 1) * H)
            ws = _bmm(loc["wq"][sl], state)
            v_new = loc["u"][sl] - ws[:, :C]
            yield None
            av = _bmm(loc["akd"][sl], v_new)
            o = ws[:, C:] + av[:, :C]
            for hd in range(H):
                o_ref[0, c * C:(c + 1) * C, hd * Dh:(hd + 1) * Dh] = o[hd]
            state = state * loc["egl"][sl] + av[:, C:]
            yield None
        yield state

    groups = [list(range(g, min(g + DN_GROUP_CHUNKS, nc))) for g in range(0, nc, DN_GROUP_CHUNKS)]
    state = s_ref[...]
    loc_prev = None
    for gi in range(len(groups) + 1):
        rec = recurrent_steps(groups[gi - 1], loc_prev, state) if gi > 0 else iter(())
        loc = None
        if gi < len(groups):
            for loc in local_steps(groups[gi]):
                out = next(rec, None)
                if out is not None:
                    state = out
        for out in rec:
            if out is not None:
                state = out
        loc_prev = loc
    s_ref[...] = state


def _dn_delta(q, k, v, gcol, grow, tt=512):
    B, S, W = q.shape
    H, Dh, C = DN_HEADS, DN_HEAD_DIM, DN_CHUNK
    tok = lambda b, j: (b, j, 0)
    spread = (jnp.arange(2 * H)[:, None] == jnp.arange(2 * W)[None, :] // Dh).astype(BF16)
    spread = jnp.concatenate([spread] * 3, axis=0)
    return pl.pallas_call(
        functools.partial(_dn_delta_kernel, tt=tt),
        grid=(B, S // tt),
        in_specs=[pl.BlockSpec((1, tt, W), tok), pl.BlockSpec((1, tt, W), tok),
                  pl.BlockSpec((1, tt, W), tok), pl.BlockSpec((1, tt, 2 * H), tok),
                  pl.BlockSpec((1, tt // C, H, C), lambda b, j: (b, j, 0, 0)),
                  _const_spec((3 * 2 * H, 2 * W))],
        out_specs=pl.BlockSpec((1, tt, W), tok),
        out_shape=jax.ShapeDtypeStruct((B, S, W), F32),
        scratch_shapes=[pltpu.VMEM((H, Dh, Dh), F32)],
        compiler_params=_params(("arbitrary", "arbitrary")),
        name="dn_delta",
    )(q, k, v, gcol, grow, spread)


def _dn_out_kernel(x_ref, o_ref, gpre_ref, wz_ref, ng_ref, wout_ref, gpost_ref, y_ref):
    H, Dh = DN_HEADS, DN_HEAD_DIM
    x = x_ref[...]
    h = _rms(x, gpre_ref[...]).astype(BF16)
    z = jnp.dot(h, wz_ref[...], preferred_element_type=F32)
    ng = ng_ref[...]
    parts = []
    for hd in range(H):
        oh = o_ref[:, hd * Dh:(hd + 1) * Dh]
        parts.append((_rms(oh, ng) * _silu(z[:, hd * Dh:(hd + 1) * Dh])).astype(BF16))
    gated = jnp.concatenate(parts, axis=-1)
    m = jnp.dot(gated, wout_ref[...], preferred_element_type=F32)
    y_ref[...] = x + _rms(m, gpost_ref[...])


def _dn_out(x2, o2, gpre, w_in, z_block, ng, wout, gpost, tm=512):
    T, D = x2.shape
    W = o2.shape[1]
    row = lambda i: (i, 0)
    return pl.pallas_call(
        _dn_out_kernel,
        grid=(T // tm,),
        in_specs=[pl.BlockSpec((tm, D), row), pl.BlockSpec((tm, W), row),
                  _const_spec((1, D)), _col_block_spec(D, W, z_block), _const_spec((1, DN_HEAD_DIM)),
                  _const_spec((W, D)), _const_spec((1, D))],
        out_specs=pl.BlockSpec((tm, D), row),
        out_shape=jax.ShapeDtypeStruct((T, D), F32),
        compiler_params=_params(("parallel",)),
        name="dn_out",
    )(x2, o2, gpre, w_in, ng, wout, gpost)


def _sg_kernel(x_ref, gpre_ref, win_ref, bin_ref, lng_ref, lnb_ref, ws_ref, bst_ref, wout_ref,
               gpost_ref, y_ref, *, tm):
    G, C = SG_GROUPS, SG_CHUNK
    E = win_ref.shape[1] // 2
    Eg = E // G
    x = x_ref[...]
    h = _rms(x, gpre_ref[...]).astype(BF16)

    def in_proj(col0):
        cols = slice(col0, col0 + Eg)
        z = jnp.dot(h, win_ref[:, cols], preferred_element_type=F32) + bin_ref[:, cols]
        return 0.5 * z * (1.0 + lax.erf(z * (2.0 ** -0.5)))

    vs = [in_proj(E + g * Eg) for g in range(G)]
    us = [in_proj(g * Eg) for g in range(G)]
    tot = vs[0]
    for g in range(1, G):
        tot = tot + vs[g]
    mu = jnp.sum(tot, axis=-1, keepdims=True) * (1.0 / E)
    xcs = [v - mu for v in vs]
    tot = xcs[0] * xcs[0]
    for g in range(1, G):
        tot = tot + xcs[g] * xcs[g]
    rstd = lax.rsqrt(jnp.sum(tot, axis=-1, keepdims=True) * (1.0 / E) + LN_EPS)

    ri = lax.broadcasted_iota(jnp.int32, (C, C), 0)
    ci = lax.broadcasted_iota(jnp.int32, (C, C), 1)
    bst = bst_ref[...]
    parts = []
    for g in range(G):
        cols = slice(g * Eg, (g + 1) * Eg)
        vn = (xcs[g] * rstd * lng_ref[:, cols] + lnb_ref[:, cols]).astype(BF16)
        wc = jnp.where(ci <= ri, ws_ref[g], 0.0).astype(BF16)
        rows = []
        for c in range(tm // C):
            mixed = jnp.dot(wc, vn[c * C:(c + 1) * C], preferred_element_type=F32) + bst[:, g:g + 1]
            rows.append((us[g][c * C:(c + 1) * C] * mixed).astype(BF16))
        parts.append(jnp.concatenate(rows, axis=0))
    gated = jnp.concatenate(parts, axis=-1)
    m = jnp.dot(gated, wout_ref[...], preferred_element_type=F32)
    y_ref[...] = x + _rms(m, gpost_ref[...])


def _sg(x2, gpre, win, b_in, lng, lnb, ws, bst, wout, gpost, tm=512):
    T, D = x2.shape
    E2 = win.shape[1]
    E = E2 // 2
    G, C = SG_GROUPS, SG_CHUNK
    row = lambda i: (i, 0)
    return pl.pallas_call(
        functools.partial(_sg_kernel, tm=tm),
        grid=(T // tm,),
        in_specs=[pl.BlockSpec((tm, D), row), _const_spec((1, D)), _const_spec((D, E2)),
                  _const_spec((1, E2)), _const_spec((1, E)), _const_spec((1, E)),
                  _const_spec((G, C, C)), _const_spec((C, G)), _const_spec((E, D)),
                  _const_spec((1, D))],
        out_specs=pl.BlockSpec((tm, D), row),
        out_shape=jax.ShapeDtypeStruct((T, D), F32),
        compiler_params=_params(("parallel",)),
        name="sg",
    )(x2, gpre, win, b_in, lng, lnb, ws, bst, wout, gpost)


def kernel(x, norm_g, ffn_w_gate, ffn_w_up, ffn_w_down, dn_w_in, dn_conv_w, dn_a_log, dn_dt_bias,
           dn_norm_g, dn_w_out, sg_w_in, sg_b_in, sg_ln_g, sg_ln_b, sg_w_s, sg_b_s, sg_w_out):
    B, S, D = x.shape
    depth = norm_g.shape[0]
    H = DN_HEADS
    W = H * DN_HEAD_DIM
    wg = ffn_w_gate.astype(BF16)
    wu = ffn_w_up.astype(BF16)
    wd = ffn_w_down.astype(BF16)
    x2 = x.reshape(B * S, D)
    for i in range(depth):
        ng = norm_g[i].reshape(6, 1, D)
        x2 = _ffn(x2, ng[0], wg[i, 0], wu[i, 0], wd[i, 0], ng[1])
        j = i // 2
        if i % 2 == 0:
            w_in = dn_w_in[j].astype(BF16)
            q, k, v, gcol, grow = _dn_in(x2.reshape(B, S, D), ng[2], w_in,
                                         w_in[:, 4 * W:], w_in[:, 4 * W:].T, dn_conv_w[j],
                                         dn_a_log[j], dn_dt_bias[j])
            o = _dn_delta(q, k, v, gcol, grow)
            x2 = _dn_out(x2, o.reshape(B * S, W), ng[2], w_in, 3,
                         dn_norm_g[j].reshape(1, -1), dn_w_out[j].astype(BF16), ng[3])
        else:
            E = sg_w_in.shape[2] // 2
            x2 = _sg(x2, ng[2], sg_w_in[j].astype(BF16), sg_b_in[j].reshape(1, -1),
                     sg_ln_g[j].reshape(1, E), sg_ln_b[j].reshape(1, E), sg_w_s[j], sg_b_s[j].T,
                     sg_w_out[j].astype(BF16), ng[3])
        x2 = _ffn(x2, ng[4], wg[i, 1], wu[i, 1], wd[i, 1], ng[5])
    return x2.reshape(B, S, D)
```

```python
import functools
import math

import jax
import jax.numpy as jnp
from jax import lax
from jax.experimental import pallas as pl
from jax.experimental.pallas import tpu as pltpu

RMS_EPS = 1e-6
LN_EPS = 1e-5
L2_EPS = 1e-6
DN_HEADS = 8
DN_HEAD_DIM = 128
DN_CONV = 4
DN_CHUNK = 64
SUBLANES = 8
DN_GROUP_CHUNKS = 4
SG_GROUPS = 8
SG_CHUNK = 128

VMEM_LIMIT_BYTES = 56 * 1024 * 1024

BF16 = jnp.bfloat16
F32 = jnp.float32


def _mm(a, b):
    return jnp.dot(a.astype(BF16), b.astype(BF16), preferred_element_type=F32)


def _mm_nt(a, b):
    return lax.dot_general(a.astype(BF16), b.astype(BF16), (((1,), (1,)), ((), ())),
                           preferred_element_type=F32)


def _mm_tn(a, b):
    return lax.dot_general(a.astype(BF16), b.astype(BF16), (((0,), (0,)), ((), ())),
                           preferred_element_type=F32)


def _bf16_terms(x):
    hi = x.astype(BF16)
    r1 = x - hi.astype(F32)
    mid = r1.astype(BF16)
    lo = (r1 - mid.astype(F32)).astype(BF16)
    return [hi, mid, lo]


def _rms(x, g):
    return x * lax.rsqrt(jnp.mean(x * x, axis=-1, keepdims=True) + RMS_EPS) * g


def _silu(x):
    return x * jax.nn.sigmoid(x)


def _const_spec(shape):
    nd = len(shape)
    return pl.BlockSpec(shape, lambda *_: (0,) * nd, pipeline_mode=pl.Buffered(1))


def _col_block_spec(rows, width, block):
    return pl.BlockSpec((rows, width), lambda *_: (0, block), pipeline_mode=pl.Buffered(1))


def _params(sem):
    return pltpu.CompilerParams(dimension_semantics=sem, vmem_limit_bytes=VMEM_LIMIT_BYTES)


def _ffn_kernel(x_ref, gpre_ref, wg_ref, wu_ref, wd_ref, gpost_ref, o_ref, *, halves):
    rows = x_ref.shape[0] // halves
    sls = [slice(i * rows, (i + 1) * rows) for i in range(halves)]
    hs = [_rms(x_ref[sl, :], gpre_ref[...]).astype(BF16) for sl in sls]
    ys = []
    for i, sl in enumerate(sls):
        gate = jnp.dot(hs[i], wg_ref[...], preferred_element_type=F32)
        up = jnp.dot(hs[i], wu_ref[...], preferred_element_type=F32)
        a = (_silu(gate) * up).astype(BF16)
        ys.append(jnp.dot(a, wd_ref[...], preferred_element_type=F32))
        if i > 0:
            o_ref[sls[i - 1], :] = x_ref[sls[i - 1], :] + 0.5 * _rms(ys[i - 1], gpost_ref[...])
    o_ref[sls[-1], :] = x_ref[sls[-1], :] + 0.5 * _rms(ys[-1], gpost_ref[...])


def _ffn(x2, gpre, wg, wu, wd, gpost, layer, slot, tm=1024, halves=2):
    T, D = x2.shape
    F = wg.shape[-1]
    pick = lambda r, c: pl.BlockSpec((None, None, r, c), lambda *_: (layer, slot, 0, 0),
                                     pipeline_mode=pl.Buffered(1))
    return pl.pallas_call(
        functools.partial(_ffn_kernel, halves=halves),
        grid=(T // tm,),
        in_specs=[pl.BlockSpec((tm, D), lambda i: (i, 0)),
                  _const_spec((1, D)), pick(D, F), pick(D, F), pick(F, D), _const_spec((1, D))],
        out_specs=pl.BlockSpec((tm, D), lambda i: (i, 0)),
        out_shape=jax.ShapeDtypeStruct((T, D), F32),
        compiler_params=_params(("parallel",)),
        name="ffn",
    )(x2, gpre, wg, wu, wd, gpost)


def _dn_in_kernel(x_ref, gpre_ref, wqkv_ref, wba_ref, wbat_ref, conv_ref, alog_c_ref, dtb_c_ref,
                  alog_r_ref, dtb_r_ref, perm_ref, cum_ref, cumt_ref,
                  q_ref, k_ref, v_ref, gcol_ref, grow_ref, *carry_refs, tm):
    H, Dh, C, P = DN_HEADS, DN_HEAD_DIM, DN_CHUNK, DN_CONV
    bw = 2 * Dh
    nb = H * Dh // bw
    Q = tm // P

    @pl.when(pl.program_id(1) == 0)
    def _():
        for carry in carry_refs:
            carry[...] = jnp.zeros(carry.shape, F32)

    x = x_ref[0]
    h_nat = _rms(x, gpre_ref[...]).astype(BF16)
    h = jnp.dot(perm_ref[...], h_nat, preferred_element_type=F32).astype(BF16)
    n_blocks = len(carry_refs)

    def project(cb):
        cols = slice(cb * bw, (cb + 1) * bw)
        return jnp.dot(h, wqkv_ref[:, cols], preferred_element_type=F32)

    ba = jnp.dot(h_nat, wba_ref[...], preferred_element_type=F32)
    bat = lax.dot_general(wbat_ref[...], h_nat, (((1,), (1,)), ((), ())),
                          preferred_element_type=F32)
    projs = {0: project(0), 1: project(1)}
    beta_c = jax.nn.sigmoid(ba[:, :H])
    g_c = -jnp.exp(alog_c_ref[...]) * jax.nn.softplus(ba[:, H:] + dtb_c_ref[...])
    g_r = -jnp.exp(alog_r_ref[...]) * jax.nn.softplus(bat[H:, :] + dtb_r_ref[...])
    gc_c = sum(jnp.dot(cum_ref[...], t, preferred_element_type=F32) for t in _bf16_terms(g_c))
    gc_r3 = jnp.dot(jnp.concatenate(_bf16_terms(g_r), axis=0), cumt_ref[...],
                    preferred_element_type=F32)
    gc_r = gc_r3[0:H] + gc_r3[H:2 * H] + gc_r3[2 * H:3 * H]
    gcol_ref[0, :, 0:H] = beta_c
    gcol_ref[0, :, H:2 * H] = gc_c
    for c in range(tm // C):
        grow_ref[0, c] = gc_r[:, c * C:(c + 1) * C]

    for cb, carry in enumerate(carry_refs):
        cols = slice(cb * bw, (cb + 1) * bw)
        proj = projs.pop(cb)
        cw = conv_ref[:, cols]
        phase = [proj[p * Q:(p + 1) * Q] for p in range(P)]
        up = {p: jnp.concatenate([carry[p - 1], phase[p]], axis=0)[7:7 + Q] for p in range(1, P)}
        for p in range(1, P):
            carry[p - 1] = phase[p][Q - 8:Q]
        ys = []
        for p in range(P):
            acc = phase[p] * cw[P - 1:P, :]
            for s in range(1, P):
                src = phase[p - s] if p >= s else up[p - s + P]
                acc = acc + src * cw[P - 1 - s:P - s, :]
            ys.append(_silu(acc))
        if cb + 2 < n_blocks:
            projs[cb + 2] = project(cb + 2)
        out_ref = (q_ref, k_ref, v_ref)[cb // nb]
        for i in range(bw // Dh):
            hd = (cb % nb) * (bw // Dh) + i
            for p in range(P):
                yh = ys[p][:, i * Dh:(i + 1) * Dh]
                if cb < 2 * nb:
                    scale = Dh ** -0.5 if cb < nb else 1.0
                    yh = yh * (lax.rsqrt(jnp.sum(yh * yh, axis=-1, keepdims=True) + L2_EPS) * scale)
                out_ref[0, hd, pl.ds(p, Q, stride=P), :] = yh


def _dn_in(x, gpre, wqkv, wba, wbat, conv_w, alog, dtb, tm=256):
    B, S, D = x.shape
    H, Dh, C, P = DN_HEADS, DN_HEAD_DIM, DN_CHUNK, DN_CONV
    W = H * Dh
    bw = 2 * Dh
    Q = tm // P
    nat = jnp.arange(tm)
    perm = ((P * (nat % Q) + nat // Q)[:, None] == nat[None, :]).astype(BF16)
    cum = ((nat[:, None] // C == nat[None, :] // C) & (nat[None, :] <= nat[:, None])).astype(BF16)
    head_major = lambda b, j: (b, 0, j, 0)
    return pl.pallas_call(
        functools.partial(_dn_in_kernel, tm=tm),
        grid=(B, S // tm),
        in_specs=[pl.BlockSpec((1, tm, D), lambda b, j: (b, j, 0)),
                  _const_spec((1, D)), _col_block_spec(D, 3 * W, 0), _const_spec((D, 2 * H)),
                  _const_spec((2 * H, D)), _const_spec((P, 3 * W)),
                  _const_spec((1, H)), _const_spec((1, H)), _const_spec((H, 1)), _const_spec((H, 1)),
                  _const_spec((tm, tm)), _const_spec((tm, tm)), _const_spec((tm, tm))],
        out_specs=[pl.BlockSpec((1, H, tm, Dh), head_major), pl.BlockSpec((1, H, tm, Dh), head_major),
                   pl.BlockSpec((1, H, tm, Dh), head_major),
                   pl.BlockSpec((1, tm, 2 * H), lambda b, j: (b, j, 0)),
                   pl.BlockSpec((1, tm // C, H, C), lambda b, j: (b, j, 0, 0))],
        out_shape=[jax.ShapeDtypeStruct((B, H, S, Dh), F32), jax.ShapeDtypeStruct((B, H, S, Dh), F32),
                   jax.ShapeDtypeStruct((B, H, S, Dh), F32), jax.ShapeDtypeStruct((B, S, 2 * H), F32),
                   jax.ShapeDtypeStruct((B, S // C, H, C), F32)],
        scratch_shapes=[pltpu.VMEM((P - 1, 8, bw), F32) for _ in range(3 * W // bw)],
        compiler_params=_params(("arbitrary", "arbitrary")),
        name="dn_in",
    )(x, gpre, wqkv, wba, wbat, conv_w, alog.reshape(1, H), dtb.reshape(1, H),
      alog.reshape(H, 1), dtb.reshape(H, 1), perm, cum, cum.T)


def _bmm(a, b):
    return lax.dot_general(a.astype(BF16), b.astype(BF16), (((2,), (1,)), ((0,), (0,))),
                           preferred_element_type=F32)


def _bmm_nt(a, b):
    return lax.dot_general(a.astype(BF16), b.astype(BF16), (((2,), (2,)), ((0,), (0,))),
                           preferred_element_type=F32)


def _bmm_tn(a, b):
    return lax.dot_general(a.astype(BF16), b.astype(BF16), (((1,), (1,)), ((0,), (0,))),
                           preferred_element_type=F32)


def _unit_lower_inverse_steps(low, ri, ci):
    C = low.shape[-1]
    eye = jnp.where(ri == ci, 1.0, 0.0).astype(F32)
    t = eye - jnp.where((ri // 2) == (ci // 2), low, 0.0)
    b = 2
    while b < C:
        off = jnp.where(((ri // (2 * b)) == (ci // (2 * b))) & ((ri // b) != (ci // b)), low, 0.0)
        if b < SUBLANES:
            t = t - _bmm(t, _bmm(off, t))
        else:
            lower = [slice(s, s + b) for s in range(b, C, 2 * b)]
            rows = lambda a: jnp.concatenate([a[:, r] for r in lower], axis=1)
            x_lo = _bmm(rows(off), t)
            zero = jnp.zeros_like(x_lo[:, :b])
            x = jnp.concatenate([p for i in range(len(lower))
                                 for p in (zero, x_lo[:, i * b:(i + 1) * b])], axis=1)
            corr = _bmm(rows(t), x)
            t = jnp.concatenate([p for i, r in enumerate(lower)
                                 for p in (t[:, r.start - b:r.start],
                                           t[:, r] - corr[:, i * b:(i + 1) * b])], axis=1)
        b *= 2
        yield None
    yield t


def _dn_delta_kernel(q_ref, k_ref, v_ref, gcol_ref, grow_ref, spread_ref, o_ref, s_ref, *, tt):
    H, Dh, C = DN_HEADS, DN_HEAD_DIM, DN_CHUNK
    nc = tt // C

    @pl.when(pl.program_id(1) == 0)
    def _():
        s_ref[...] = jnp.zeros_like(s_ref)

    ri = lax.broadcasted_iota(jnp.int32, (1, C, C), 1)
    ci = lax.broadcasted_iota(jnp.int32, (1, C, C), 2)
    causal = ci <= ri
    strict = ci < ri

    gates = gcol_ref[0]
    spread = jnp.dot(jnp.concatenate(_bf16_terms(gates), axis=1), spread_ref[...],
                     preferred_element_type=F32)

    def local_steps(chunks):
        per_head = lambda ref: jnp.stack([ref[0, hd, c * C:(c + 1) * C, :]
                                          for c in chunks for hd in range(H)])
        q, k, v = per_head(q_ref), per_head(k_ref), per_head(v_ref)
        beta = jnp.stack([spread[c * C:(c + 1) * C, hd * Dh:(hd + 1) * Dh]
                          for c in chunks for hd in range(H)])
        gc = jnp.stack([spread[c * C:(c + 1) * C, (H + hd) * Dh:(H + hd + 1) * Dh]
                        for c in chunks for hd in range(H)])
        gr = jnp.stack([grow_ref[0, c, hd:hd + 1, :] for c in chunks for hd in range(H)])
        gl = gc[:, C - 1:C, :]
        decay = jnp.where(causal, jnp.exp(jnp.where(causal, gc[:, :, :C] - gr, 0.0)), 0.0)
        egc = jnp.exp(gc)
        kb = k * beta
        kq = _bmm_nt(jnp.concatenate([kb, q], axis=1), k)
        yield None
        low = jnp.where(strict, kq[:, :C] * decay, 0.0)
        attn = kq[:, C:] * decay
        t = None
        for t in _unit_lower_inverse_steps(low, ri, ci):
            if t is None:
                yield None
        uw = _bmm(t, jnp.concatenate([v * beta, kb * egc], axis=2))
        wq = jnp.concatenate([uw[:, :, Dh:], q * egc], axis=1).astype(BF16)
        kdt = jnp.swapaxes(k * jnp.exp(gl - gc), 1, 2)
        akd = jnp.concatenate([attn, kdt], axis=1).astype(BF16)
        yield dict(u=uw[:, :, :Dh], wq=wq, akd=akd, egl=jnp.exp(gl))

    def recurrent_steps(chunks, loc, state):
        for i, c in enumerate(chunks):
            sl = slice(i * H, (i + 1) * H)
            ws = _bmm(loc["wq"][sl], state)
            v_new = loc["u"][sl] - ws[:, :C]
            yield None
            av = _bmm(loc["akd"][sl], v_new)
            o = ws[:, C:] + av[:, :C]
            for hd in range(H):
                o_ref[0, c * C:(c + 1) * C, hd * Dh:(hd + 1) * Dh] = o[hd]
            state = state * loc["egl"][sl] + av[:, C:]
            yield None
        yield state

    groups = [list(range(g, min(g + DN_GROUP_CHUNKS, nc))) for g in range(0, nc, DN_GROUP_CHUNKS)]
    state = s_ref[...]
    loc_prev = None
    for gi in range(len(groups) + 1):
        rec = recurrent_steps(groups[gi - 1], loc_prev, state) if gi > 0 else iter(())
        loc = None
        if gi < len(groups):
            for loc in local_steps(groups[gi]):
                out = next(rec, None)
                if out is not None:
                    state = out
        for out in rec:
            if out is not None:
                state = out
        loc_prev = loc
    s_ref[...] = state


def _dn_delta(q, k, v, gcol, grow, tt=512):
    B, H, S, Dh = q.shape
    C = DN_CHUNK
    W = H * Dh
    tok = lambda b, j: (b, j, 0)
    head_major = pl.BlockSpec((1, H, tt, Dh), lambda b, j: (b, 0, j, 0))
    spread = (jnp.arange(2 * H)[:, None] == jnp.arange(2 * W)[None, :] // Dh).astype(BF16)
    spread = jnp.concatenate([spread] * 3, axis=0)
    return pl.pallas_call(
        functools.partial(_dn_delta_kernel, tt=tt),
        grid=(B, S // tt),
        in_specs=[head_major, head_major, head_major, pl.BlockSpec((1, tt, 2 * H), tok),
                  pl.BlockSpec((1, tt // C, H, C), lambda b, j: (b, j, 0, 0)),
                  _const_spec((3 * 2 * H, 2 * W))],
        out_specs=pl.BlockSpec((1, tt, W), tok),
        out_shape=jax.ShapeDtypeStruct((B, S, W), F32),
        scratch_shapes=[pltpu.VMEM((H, Dh, Dh), F32)],
        compiler_params=_params(("arbitrary", "arbitrary")),
        name="dn_delta",
    )(q, k, v, gcol, grow, spread)


def _dn_out_kernel(x_ref, o_ref, gpre_ref, wz_ref, ng_ref, wout_ref, gpost_ref, y_ref):
    H, Dh = DN_HEADS, DN_HEAD_DIM
    x = x_ref[...]
    h = _rms(x, gpre_ref[...]).astype(BF16)
    z = jnp.dot(h, wz_ref[...], preferred_element_type=F32)
    ng = ng_ref[...]
    parts = []
    for hd in range(H):
        oh = o_ref[:, hd * Dh:(hd + 1) * Dh]
        parts.append((_rms(oh, ng) * _silu(z[:, hd * Dh:(hd + 1) * Dh])).astype(BF16))
    gated = jnp.concatenate(parts, axis=-1)
    m = jnp.dot(gated, wout_ref[...], preferred_element_type=F32)
    y_ref[...] = x + _rms(m, gpost_ref[...])


def _dn_out(x2, o2, gpre, w_in, z_block, ng, wout, gpost, tm=512):
    T, D = x2.shape
    W = o2.shape[1]
    row = lambda i: (i, 0)
    return pl.pallas_call(
        _dn_out_kernel,
        grid=(T // tm,),
        in_specs=[pl.BlockSpec((tm, D), row), pl.BlockSpec((tm, W), row),
                  _const_spec((1, D)), _col_block_spec(D, W, z_block), _const_spec((1, DN_HEAD_DIM)),
                  _const_spec((W, D)), _const_spec((1, D))],
        out_specs=pl.BlockSpec((tm, D), row),
        out_shape=jax.ShapeDtypeStruct((T, D), F32),
        compiler_params=_params(("parallel",)),
        name="dn_out",
    )(x2, o2, gpre, w_in, ng, wout, gpost)


def _sg_kernel(x_ref, gpre_ref, win_ref, bin_ref, lng_ref, lnb_ref, ws_ref, bst_ref, wout_ref,
               gpost_ref, y_ref, *, tm):
    G, C = SG_GROUPS, SG_CHUNK
    E = win_ref.shape[1] // 2
    Eg = E // G
    x = x_ref[...]
    h = _rms(x, gpre_ref[...]).astype(BF16)

    def in_proj(col0):
        cols = slice(col0, col0 + Eg)
        z = jnp.dot(h, win_ref[:, cols], preferred_element_type=F32) + bin_ref[:, cols]
        return 0.5 * z * (1.0 + lax.erf(z * (2.0 ** -0.5)))

    vs = [in_proj(E + g * Eg) for g in range(G)]
    us = [in_proj(g * Eg) for g in range(G)]
    tot = vs[0]
    for g in range(1, G):
        tot = tot + vs[g]
    mu = jnp.sum(tot, axis=-1, keepdims=True) * (1.0 / E)
    xcs = [v - mu for v in vs]
    tot = xcs[0] * xcs[0]
    for g in range(1, G):
        tot = tot + xcs[g] * xcs[g]
    rstd = lax.rsqrt(jnp.sum(tot, axis=-1, keepdims=True) * (1.0 / E) + LN_EPS)

    ri = lax.broadcasted_iota(jnp.int32, (C, C), 0)
    ci = lax.broadcasted_iota(jnp.int32, (C, C), 1)
    bst = bst_ref[...]
    parts = []
    for g in range(G):
        cols = slice(g * Eg, (g + 1) * Eg)
        vn = (xcs[g] * rstd * lng_ref[:, cols] + lnb_ref[:, cols]).astype(BF16)
        wc = jnp.where(ci <= ri, ws_ref[g], 0.0).astype(BF16)
        rows = []
        for c in range(tm // C):
            mixed = jnp.dot(wc, vn[c * C:(c + 1) * C], preferred_element_type=F32) + bst[:, g:g + 1]
            rows.append((us[g][c * C:(c + 1) * C] * mixed).astype(BF16))
        parts.append(jnp.concatenate(rows, axis=0))
    gated = jnp.concatenate(parts, axis=-1)
    m = jnp.dot(gated, wout_ref[...], preferred_element_type=F32)
    y_ref[...] = x + _rms(m, gpost_ref[...])


def _sg(x2, gpre, win, b_in, lng, lnb, ws, bst, wout, gpost, tm=512):
    T, D = x2.shape
    E2 = win.shape[1]
    E = E2 // 2
    G, C = SG_GROUPS, SG_CHUNK
    row = lambda i: (i, 0)
    return pl.pallas_call(
        functools.partial(_sg_kernel, tm=tm),
        grid=(T // tm,),
        in_specs=[pl.BlockSpec((tm, D), row), _const_spec((1, D)), _const_spec((D, E2)),
                  _const_spec((1, E2)), _const_spec((1, E)), _const_spec((1, E)),
                  _const_spec((G, C, C)), _const_spec((C, G)), _const_spec((E, D)),
                  _const_spec((1, D))],
        out_specs=pl.BlockSpec((tm, D), row),
        out_shape=jax.ShapeDtypeStruct((T, D), F32),
        compiler_params=_params(("parallel",)),
        name="sg",
    )(x2, gpre, win, b_in, lng, lnb, ws, bst, wout, gpost)


def kernel(x, norm_g, ffn_w_gate, ffn_w_up, ffn_w_down, dn_w_in, dn_conv_w, dn_a_log, dn_dt_bias,
           dn_norm_g, dn_w_out, sg_w_in, sg_b_in, sg_ln_g, sg_ln_b, sg_w_s, sg_b_s, sg_w_out):
    B, S, D = x.shape
    depth = norm_g.shape[0]
    H = DN_HEADS
    W = H * DN_HEAD_DIM
    wg = ffn_w_gate.astype(BF16)
    wu = ffn_w_up.astype(BF16)
    wd = ffn_w_down.astype(BF16)
    x2 = x.reshape(B * S, D)
    for i in range(depth):
        ng = norm_g[i].reshape(6, 1, D)
        x2 = _ffn(x2, ng[0], wg, wu, wd, ng[1], i, 0)
        j = i // 2
        if i % 2 == 0:
            w_in = dn_w_in[j].astype(BF16)
            q, k, v, gcol, grow = _dn_in(x2.reshape(B, S, D), ng[2], w_in,
                                         w_in[:, 4 * W:], w_in[:, 4 * W:].T, dn_conv_w[j],
                                         dn_a_log[j], dn_dt_bias[j])
            o = _dn_delta(q, k, v, gcol, grow)
            x2 = _dn_out(x2, o.reshape(B * S, W), ng[2], w_in, 3,
                         dn_norm_g[j].reshape(1, -1), dn_w_out[j].astype(BF16), ng[3])
        else:
            E = sg_w_in.shape[2] // 2
            x2 = _sg(x2, ng[2], sg_w_in[j].astype(BF16), sg_b_in[j].reshape(1, -1),
                     sg_ln_g[j].reshape(1, E), sg_ln_b[j].reshape(1, E), sg_w_s[j], sg_b_s[j].T,
                     sg_w_out[j].astype(BF16), ng[3])
        x2 = _ffn(x2, ng[4], wg, wu, wd, ng[5], i, 1)
    return x2.reshape(B, S, D)
```

```python
import functools
import math

import jax
import jax.numpy as jnp
from jax import lax
from jax.experimental import pallas as pl
from jax.experimental.pallas import tpu as pltpu

RMS_EPS = 1e-6
LN_EPS = 1e-5
L2_EPS = 1e-6
DN_HEADS = 8
DN_HEAD_DIM = 128
DN_CONV = 4
DN_CHUNK = 64
SUBLANES = 8
DN_GROUP_CHUNKS = 4
SG_GROUPS = 8
SG_CHUNK = 128

VMEM_LIMIT_BYTES = 56 * 1024 * 1024

BF16 = jnp.bfloat16
F32 = jnp.float32


def _mm(a, b):
    return jnp.dot(a.astype(BF16), b.astype(BF16), preferred_element_type=F32)


def _mm_nt(a, b):
    return lax.dot_general(a.astype(BF16), b.astype(BF16), (((1,), (1,)), ((), ())),
                           preferred_element_type=F32)


def _mm_tn(a, b):
    return lax.dot_general(a.astype(BF16), b.astype(BF16), (((0,), (0,)), ((), ())),
                           preferred_element_type=F32)


def _bf16_terms(x):
    hi = x.astype(BF16)
    r1 = x - hi.astype(F32)
    mid = r1.astype(BF16)
    lo = (r1 - mid.astype(F32)).astype(BF16)
    return [hi, mid, lo]


def _rms(x, g):
    return x * lax.rsqrt(jnp.mean(x * x, axis=-1, keepdims=True) + RMS_EPS) * g


def _silu(x):
    return x * jax.nn.sigmoid(x)


def _const_spec(shape):
    nd = len(shape)
    return pl.BlockSpec(shape, lambda *_: (0,) * nd, pipeline_mode=pl.Buffered(1))


def _col_block_spec(rows, width, block):
    return pl.BlockSpec((rows, width), lambda *_: (0, block), pipeline_mode=pl.Buffered(1))


def _params(sem):
    return pltpu.CompilerParams(dimension_semantics=sem, vmem_limit_bytes=VMEM_LIMIT_BYTES)


def _ffn_kernel(x_ref, gpre_ref, wg_ref, wu_ref, wd_ref, gpost_ref, o_ref, *, halves):
    rows = x_ref.shape[0] // halves
    sls = [slice(i * rows, (i + 1) * rows) for i in range(halves)]
    hs = [_rms(x_ref[sl, :], gpre_ref[...]).astype(BF16) for sl in sls]
    ys = []
    for i, sl in enumerate(sls):
        gate = jnp.dot(hs[i], wg_ref[...], preferred_element_type=F32)
        up = jnp.dot(hs[i], wu_ref[...], preferred_element_type=F32)
        a = (_silu(gate) * up).astype(BF16)
        ys.append(jnp.dot(a, wd_ref[...], preferred_element_type=F32))
        if i > 0:
            o_ref[sls[i - 1], :] = x_ref[sls[i - 1], :] + 0.5 * _rms(ys[i - 1], gpost_ref[...])
    o_ref[sls[-1], :] = x_ref[sls[-1], :] + 0.5 * _rms(ys[-1], gpost_ref[...])


def _ffn(x2, gpre, wg, wu, wd, gpost, layer, slot, tm=1024, halves=4):
    T, D = x2.shape
    F = wg.shape[-1]
    pick = lambda r, c: pl.BlockSpec((None, None, r, c), lambda *_: (layer, slot, 0, 0),
                                     pipeline_mode=pl.Buffered(1))
    return pl.pallas_call(
        functools.partial(_ffn_kernel, halves=halves),
        grid=(T // tm,),
        in_specs=[pl.BlockSpec((tm, D), lambda i: (i, 0)),
                  _const_spec((1, D)), pick(D, F), pick(D, F), pick(F, D), _const_spec((1, D))],
        out_specs=pl.BlockSpec((tm, D), lambda i: (i, 0)),
        out_shape=jax.ShapeDtypeStruct((T, D), F32),
        compiler_params=_params(("parallel",)),
        name="ffn",
    )(x2, gpre, wg, wu, wd, gpost)


def _dn_in_kernel(x_ref, gpre_ref, wqkv_ref, wba_ref, wbat_ref, conv_ref, alog_c_ref, dtb_c_ref,
                  alog_r_ref, dtb_r_ref, perm_ref, cum_ref, cumt_ref,
                  q_ref, k_ref, v_ref, gcol_ref, grow_ref, *carry_refs, tm):
    H, Dh, C, P = DN_HEADS, DN_HEAD_DIM, DN_CHUNK, DN_CONV
    bw = 2 * Dh
    nb = H * Dh // bw
    Q = tm // P

    @pl.when(pl.program_id(1) == 0)
    def _():
        for carry in carry_refs:
            carry[...] = jnp.zeros(carry.shape, F32)

    x = x_ref[0]
    h_nat = _rms(x, gpre_ref[...]).astype(BF16)
    h = jnp.dot(perm_ref[...], h_nat, preferred_element_type=F32).astype(BF16)
    n_blocks = len(carry_refs)

    def project(cb):
        cols = slice(cb * bw, (cb + 1) * bw)
        return jnp.dot(h, wqkv_ref[:, cols], preferred_element_type=F32)

    ba = jnp.dot(h_nat, wba_ref[...], preferred_element_type=F32)
    bat = lax.dot_general(wbat_ref[...], h_nat, (((1,), (1,)), ((), ())),
                          preferred_element_type=F32)
    projs = {0: project(0), 1: project(1)}
    beta_c = jax.nn.sigmoid(ba[:, :H])
    g_c = -jnp.exp(alog_c_ref[...]) * jax.nn.softplus(ba[:, H:] + dtb_c_ref[...])
    g_r = -jnp.exp(alog_r_ref[...]) * jax.nn.softplus(bat[H:, :] + dtb_r_ref[...])
    gc_c = sum(jnp.dot(cum_ref[...], t, preferred_element_type=F32) for t in _bf16_terms(g_c))
    gc_r3 = jnp.dot(jnp.concatenate(_bf16_terms(g_r), axis=0), cumt_ref[...],
                    preferred_element_type=F32)
    gc_r = gc_r3[0:H] + gc_r3[H:2 * H] + gc_r3[2 * H:3 * H]
    gcol_ref[0, :, 0:H] = beta_c
    gcol_ref[0, :, H:2 * H] = gc_c
    for c in range(tm // C):
        grow_ref[0, c] = gc_r[:, c * C:(c + 1) * C]

    for cb, carry in enumerate(carry_refs):
        cols = slice(cb * bw, (cb + 1) * bw)
        proj = projs.pop(cb)
        cw = conv_ref[:, cols]
        phase = [proj[p * Q:(p + 1) * Q] for p in range(P)]
        up = {p: jnp.concatenate([carry[p - 1], phase[p]], axis=0)[7:7 + Q] for p in range(1, P)}
        for p in range(1, P):
            carry[p - 1] = phase[p][Q - 8:Q]
        ys = []
        for p in range(P):
            acc = phase[p] * cw[P - 1:P, :]
            for s in range(1, P):
                src = phase[p - s] if p >= s else up[p - s + P]
                acc = acc + src * cw[P - 1 - s:P - s, :]
            ys.append(_silu(acc))
        if cb + 2 < n_blocks:
            projs[cb + 2] = project(cb + 2)
        out_ref = (q_ref, k_ref, v_ref)[cb // nb]
        for i in range(bw // Dh):
            hd = (cb % nb) * (bw // Dh) + i
            for p in range(P):
                yh = ys[p][:, i * Dh:(i + 1) * Dh]
                if cb < 2 * nb:
                    scale = Dh ** -0.5 if cb < nb else 1.0
                    yh = yh * (lax.rsqrt(jnp.sum(yh * yh, axis=-1, keepdims=True) + L2_EPS) * scale)
                out_ref[0, hd, pl.ds(p, Q, stride=P), :] = yh


def _dn_in(x, gpre, wqkv, wba, wbat, conv_w, alog, dtb, tm=256):
    B, S, D = x.shape
    H, Dh, C, P = DN_HEADS, DN_HEAD_DIM, DN_CHUNK, DN_CONV
    W = H * Dh
    bw = 2 * Dh
    Q = tm // P
    nat = jnp.arange(tm)
    perm = ((P * (nat % Q) + nat // Q)[:, None] == nat[None, :]).astype(BF16)
    cum = ((nat[:, None] // C == nat[None, :] // C) & (nat[None, :] <= nat[:, None])).astype(BF16)
    head_major = lambda b, j: (b, 0, j, 0)
    return pl.pallas_call(
        functools.partial(_dn_in_kernel, tm=tm),
        grid=(B, S // tm),
        in_specs=[pl.BlockSpec((1, tm, D), lambda b, j: (b, j, 0)),
                  _const_spec((1, D)), _col_block_spec(D, 3 * W, 0), _const_spec((D, 2 * H)),
                  _const_spec((2 * H, D)), _const_spec((P, 3 * W)),
                  _const_spec((1, H)), _const_spec((1, H)), _const_spec((H, 1)), _const_spec((H, 1)),
                  _const_spec((tm, tm)), _const_spec((tm, tm)), _const_spec((tm, tm))],
        out_specs=[pl.BlockSpec((1, H, tm, Dh), head_major), pl.BlockSpec((1, H, tm, Dh), head_major),
                   pl.BlockSpec((1, H, tm, Dh), head_major),
                   pl.BlockSpec((1, tm, 2 * H), lambda b, j: (b, j, 0)),
                   pl.BlockSpec((1, tm // C, H, C), lambda b, j: (b, j, 0, 0))],
        out_shape=[jax.ShapeDtypeStruct((B, H, S, Dh), F32), jax.ShapeDtypeStruct((B, H, S, Dh), F32),
                   jax.ShapeDtypeStruct((B, H, S, Dh), F32), jax.ShapeDtypeStruct((B, S, 2 * H), F32),
                   jax.ShapeDtypeStruct((B, S // C, H, C), F32)],
        scratch_shapes=[pltpu.VMEM((P - 1, 8, bw), F32) for _ in range(3 * W // bw)],
        compiler_params=_params(("arbitrary", "arbitrary")),
        name="dn_in",
    )(x, gpre, wqkv, wba, wbat, conv_w, alog.reshape(1, H), dtb.reshape(1, H),
      alog.reshape(H, 1), dtb.reshape(H, 1), perm, cum, cum.T)


def _bmm(a, b):
    return lax.dot_general(a.astype(BF16), b.astype(BF16), (((2,), (1,)), ((0,), (0,))),
                           preferred_element_type=F32)


def _bmm_nt(a, b):
    return lax.dot_general(a.astype(BF16), b.astype(BF16), (((2,), (2,)), ((0,), (0,))),
                           preferred_element_type=F32)


def _bmm_tn(a, b):
    return lax.dot_general(a.astype(BF16), b.astype(BF16), (((1,), (1,)), ((0,), (0,))),
                           preferred_element_type=F32)


def _unit_lower_inverse_steps(low, ri, ci):
    C = low.shape[-1]
    eye = jnp.where(ri == ci, 1.0, 0.0).astype(F32)
    t = eye - jnp.where((ri // 2) == (ci // 2), low, 0.0)
    b = 2
    while b < C:
        off = jnp.where(((ri // (2 * b)) == (ci // (2 * b))) & ((ri // b) != (ci // b)), low, 0.0)
        if b < SUBLANES:
            t = t - _bmm(t, _bmm(off, t))
        else:
            lower = [slice(s, s + b) for s in range(b, C, 2 * b)]
            rows = lambda a: jnp.concatenate([a[:, r] for r in lower], axis=1)
            x_lo = _bmm(rows(off), t)
            zero = jnp.zeros_like(x_lo[:, :b])
            x = jnp.concatenate([p for i in range(len(lower))
                                 for p in (zero, x_lo[:, i * b:(i + 1) * b])], axis=1)
            corr = _bmm(rows(t), x)
            t = jnp.concatenate([p for i, r in enumerate(lower)
                                 for p in (t[:, r.start - b:r.start],
                                           t[:, r] - corr[:, i * b:(i + 1) * b])], axis=1)
        b *= 2
        yield None
    yield t


def _dn_delta_kernel(q_ref, k_ref, v_ref, gcol_ref, grow_ref, spread_ref, o_ref, s_ref, *, tt):
    H, Dh, C = DN_HEADS, DN_HEAD_DIM, DN_CHUNK
    nc = tt // C

    @pl.when(pl.program_id(1) == 0)
    def _():
        s_ref[...] = jnp.zeros_like(s_ref)

    ri = lax.broadcasted_iota(jnp.int32, (1, C, C), 1)
    ci = lax.broadcasted_iota(jnp.int32, (1, C, C), 2)
    causal = ci <= ri
    strict = ci < ri

    gates = gcol_ref[0]
    spread = jnp.dot(jnp.concatenate(_bf16_terms(gates), axis=1), spread_ref[...],
                     preferred_element_type=F32)

    def local_steps(chunks):
        per_head = lambda ref: jnp.stack([ref[0, hd, c * C:(c + 1) * C, :]
                                          for c in chunks for hd in range(H)])
        q, k, v = per_head(q_ref), per_head(k_ref), per_head(v_ref)
        beta = jnp.stack([spread[c * C:(c + 1) * C, hd * Dh:(hd + 1) * Dh]
                          for c in chunks for hd in range(H)])
        gc = jnp.stack([spread[c * C:(c + 1) * C, (H + hd) * Dh:(H + hd + 1) * Dh]
                        for c in chunks for hd in range(H)])
        gr = jnp.stack([grow_ref[0, c, hd:hd + 1, :] for c in chunks for hd in range(H)])
        gl = gc[:, C - 1:C, :]
        decay = jnp.where(causal, jnp.exp(jnp.where(causal, gc[:, :, :C] - gr, 0.0)), 0.0)
        egc = jnp.exp(gc)
        kb = k * beta
        kq = _bmm_nt(jnp.concatenate([kb, q], axis=1), k)
        yield None
        low = jnp.where(strict, kq[:, :C] * decay, 0.0)
        attn = kq[:, C:] * decay
        t = None
        for t in _unit_lower_inverse_steps(low, ri, ci):
            if t is None:
                yield None
        uw = _bmm(t, jnp.concatenate([v * beta, kb * egc], axis=2))
        wq = jnp.concatenate([uw[:, :, Dh:], q * egc], axis=1).astype(BF16)
        kdt = jnp.swapaxes(k * jnp.exp(gl - gc), 1, 2)
        akd = jnp.concatenate([attn, kdt], axis=1).astype(BF16)
        yield dict(u=uw[:, :, :Dh], wq=wq, akd=akd, egl=jnp.exp(gl))

    def recurrent_steps(chunks, loc, state):
        for i, c in enumerate(chunks):
            sl = slice(i * H, (i + 1) * H)
            ws = _bmm(loc["wq"][sl], state)
            v_new = loc["u"][sl] - ws[:, :C]
            yield None
            av = _bmm(loc["akd"][sl], v_new)
            o = ws[:, C:] + av[:, :C]
            for hd in range(H):
                o_ref[0, c * C:(c + 1) * C, hd * Dh:(hd + 1) * Dh] = o[hd]
            state = state * loc["egl"][sl] + av[:, C:]
            yield None
        yield state

    groups = [list(range(g, min(g + DN_GROUP_CHUNKS, nc))) for g in range(0, nc, DN_GROUP_CHUNKS)]
    state = s_ref[...]
    loc_prev = None
    for gi in range(len(groups) + 1):
        rec = recurrent_steps(groups[gi - 1], loc_prev, state) if gi > 0 else iter(())
        loc = None
        if gi < len(groups):
            for loc in local_steps(groups[gi]):
                out = next(rec, None)
                if out is not None:
                    state = out
        for out in rec:
            if out is not None:
                state = out
        loc_prev = loc
    s_ref[...] = state


def _dn_delta(q, k, v, gcol, grow, tt=1024):
    B, H, S, Dh = q.shape
    C = DN_CHUNK
    W = H * Dh
    tok = lambda b, j: (b, j, 0)
    head_major = pl.BlockSpec((1, H, tt, Dh), lambda b, j: (b, 0, j, 0))
    spread = (jnp.arange(2 * H)[:, None] == jnp.arange(2 * W)[None, :] // Dh).astype(BF16)
    spread = jnp.concatenate([spread] * 3, axis=0)
    return pl.pallas_call(
        functools.partial(_dn_delta_kernel, tt=tt),
        grid=(B, S // tt),
        in_specs=[head_major, head_major, head_major, pl.BlockSpec((1, tt, 2 * H), tok),
                  pl.BlockSpec((1, tt // C, H, C), lambda b, j: (b, j, 0, 0)),
                  _const_spec((3 * 2 * H, 2 * W))],
        out_specs=pl.BlockSpec((1, tt, W), tok),
        out_shape=jax.ShapeDtypeStruct((B, S, W), F32),
        scratch_shapes=[pltpu.VMEM((H, Dh, Dh), F32)],
        compiler_params=_params(("arbitrary", "arbitrary")),
        name="dn_delta",
    )(q, k, v, gcol, grow, spread)


def _dn_out_kernel(x_ref, o_ref, gpre_ref, wz_ref, ng_ref, wout_ref, gpost_ref, y_ref, *, halves):
    H, Dh = DN_HEADS, DN_HEAD_DIM
    rows = x_ref.shape[0] // halves
    sls = [slice(i * rows, (i + 1) * rows) for i in range(halves)]
    ng = ng_ref[...]
    hs = [_rms(x_ref[sl, :], gpre_ref[...]).astype(BF16) for sl in sls]
    zs = [jnp.dot(h, wz_ref[...], preferred_element_type=F32) for h in hs]
    for i, sl in enumerate(sls):
        parts = []
        for hd in range(H):
            oh = o_ref[sl, hd * Dh:(hd + 1) * Dh]
            parts.append((_rms(oh, ng) * _silu(zs[i][:, hd * Dh:(hd + 1) * Dh])).astype(BF16))
        m = jnp.dot(jnp.concatenate(parts, axis=-1), wout_ref[...], preferred_element_type=F32)
        y_ref[sl, :] = x_ref[sl, :] + _rms(m, gpost_ref[...])


def _dn_out(x2, o2, gpre, w_in, z_block, ng, wout, gpost, tm=1024, halves=2):
    T, D = x2.shape
    W = o2.shape[1]
    row = lambda i: (i, 0)
    return pl.pallas_call(
        functools.partial(_dn_out_kernel, halves=halves),
        grid=(T // tm,),
        in_specs=[pl.BlockSpec((tm, D), row), pl.BlockSpec((tm, W), row),
                  _const_spec((1, D)), _col_block_spec(D, W, z_block), _const_spec((1, DN_HEAD_DIM)),
                  _const_spec((W, D)), _const_spec((1, D))],
        out_specs=pl.BlockSpec((tm, D), row),
        out_shape=jax.ShapeDtypeStruct((T, D), F32),
        compiler_params=_params(("parallel",)),
        name="dn_out",
    )(x2, o2, gpre, w_in, ng, wout, gpost)


def _sg_kernel(x_ref, gpre_ref, win_ref, bin_ref, lng_ref, lnb_ref, ws_ref, bst_ref, wout_ref,
               gpost_ref, y_ref, *, tm):
    G, C = SG_GROUPS, SG_CHUNK
    E = win_ref.shape[1] // 2
    Eg = E // G
    x = x_ref[...]
    h = _rms(x, gpre_ref[...]).astype(BF16)

    def in_proj(col0):
        cols = slice(col0, col0 + Eg)
        z = jnp.dot(h, win_ref[:, cols], preferred_element_type=F32) + bin_ref[:, cols]
        return 0.5 * z * (1.0 + lax.erf(z * (2.0 ** -0.5)))

    vs = [in_proj(E + g * Eg) for g in range(G)]
    us = [in_proj(g * Eg) for g in range(G)]
    tot = vs[0]
    for g in range(1, G):
        tot = tot + vs[g]
    mu = jnp.sum(tot, axis=-1, keepdims=True) * (1.0 / E)
    xcs = [v - mu for v in vs]
    tot = xcs[0] * xcs[0]
    for g in range(1, G):
        tot = tot + xcs[g] * xcs[g]
    rstd = lax.rsqrt(jnp.sum(tot, axis=-1, keepdims=True) * (1.0 / E) + LN_EPS)

    ri = lax.broadcasted_iota(jnp.int32, (C, C), 0)
    ci = lax.broadcasted_iota(jnp.int32, (C, C), 1)
    bst = bst_ref[...]
    parts = []
    for g in range(G):
        cols = slice(g * Eg, (g + 1) * Eg)
        vn = (xcs[g] * rstd * lng_ref[:, cols] + lnb_ref[:, cols]).astype(BF16)
        wc = jnp.where(ci <= ri, ws_ref[g], 0.0).astype(BF16)
        rows = []
        for c in range(tm // C):
            mixed = jnp.dot(wc, vn[c * C:(c + 1) * C], preferred_element_type=F32) + bst[:, g:g + 1]
            rows.append((us[g][c * C:(c + 1) * C] * mixed).astype(BF16))
        parts.append(jnp.concatenate(rows, axis=0))
    gated = jnp.concatenate(parts, axis=-1)
    m = jnp.dot(gated, wout_ref[...], preferred_element_type=F32)
    y_ref[...] = x + _rms(m, gpost_ref[...])


def _sg(x2, gpre, win, b_in, lng, lnb, ws, bst, wout, gpost, tm=1024):
    T, D = x2.shape
    E2 = win.shape[1]
    E = E2 // 2
    G, C = SG_GROUPS, SG_CHUNK
    row = lambda i: (i, 0)
    return pl.pallas_call(
        functools.partial(_sg_kernel, tm=tm),
        grid=(T // tm,),
        in_specs=[pl.BlockSpec((tm, D), row), _const_spec((1, D)), _const_spec((D, E2)),
                  _const_spec((1, E2)), _const_spec((1, E)), _const_spec((1, E)),
                  _const_spec((G, C, C)), _const_spec((C, G)), _const_spec((E, D)),
                  _const_spec((1, D))],
        out_specs=pl.BlockSpec((tm, D), row),
        out_shape=jax.ShapeDtypeStruct((T, D), F32),
        compiler_params=_params(("parallel",)),
        name="sg",
    )(x2, gpre, win, b_in, lng, lnb, ws, bst, wout, gpost)


def kernel(x, norm_g, ffn_w_gate, ffn_w_up, ffn_w_down, dn_w_in, dn_conv_w, dn_a_log, dn_dt_bias,
           dn_norm_g, dn_w_out, sg_w_in, sg_b_in, sg_ln_g, sg_ln_b, sg_w_s, sg_b_s, sg_w_out):
    B, S, D = x.shape
    depth = norm_g.shape[0]
    H = DN_HEADS
    W = H * DN_HEAD_DIM
    wg = ffn_w_gate.astype(BF16)
    wu = ffn_w_up.astype(BF16)
    wd = ffn_w_down.astype(BF16)
    x2 = x.reshape(B * S, D)
    for i in range(depth):
        ng = norm_g[i].reshape(6, 1, D)
        x2 = _ffn(x2, ng[0], wg, wu, wd, ng[1], i, 0)
        j = i // 2
        if i % 2 == 0:
            w_in = dn_w_in[j].astype(BF16)
            q, k, v, gcol, grow = _dn_in(x2.reshape(B, S, D), ng[2], w_in,
                                         w_in[:, 4 * W:], w_in[:, 4 * W:].T, dn_conv_w[j],
                                         dn_a_log[j], dn_dt_bias[j])
            o = _dn_delta(q, k, v, gcol, grow)
            x2 = _dn_out(x2, o.reshape(B * S, W), ng[2], w_in, 3,
                         dn_norm_g[j].reshape(1, -1), dn_w_out[j].astype(BF16), ng[3])
        else:
            E = sg_w_in.shape[2] // 2
            x2 = _sg(x2, ng[2], sg_w_in[j].astype(BF16), sg_b_in[j].reshape(1, -1),
                     sg_ln_g[j].reshape(1, E), sg_ln_b[j].reshape(1, E), sg_w_s[j], sg_b_s[j].T,
                     sg_w_out[j].astype(BF16), ng[3])
        x2 = _ffn(x2, ng[4], wg, wu, wd, ng[5], i, 1)
    return x2.reshape(B, S, D)
```

```python
import functools
import math

import jax
import jax.numpy as jnp
from jax import lax
from jax.experimental import pallas as pl
from jax.experimental.pallas import tpu as pltpu

RMS_EPS = 1e-6
LN_EPS = 1e-5
L2_EPS = 1e-6
DN_HEADS = 8
DN_HEAD_DIM = 128
DN_CONV = 4
DN_CHUNK = 64
SUBLANES = 8
BF16_SUBLANES = 16
DN_GROUP_CHUNKS = 4
SG_GROUPS = 8
SG_CHUNK = 128

VMEM_LIMIT_BYTES = 56 * 1024 * 1024

BF16 = jnp.bfloat16
F32 = jnp.float32


def _mm(a, b):
    return jnp.dot(a.astype(BF16), b.astype(BF16), preferred_element_type=F32)


def _mm_nt(a, b):
    return lax.dot_general(a.astype(BF16), b.astype(BF16), (((1,), (1,)), ((), ())),
                           preferred_element_type=F32)


def _mm_tn(a, b):
    return lax.dot_general(a.astype(BF16), b.astype(BF16), (((0,), (0,)), ((), ())),
                           preferred_element_type=F32)


def _bf16_terms(x):
    hi = x.astype(BF16)
    r1 = x - hi.astype(F32)
    mid = r1.astype(BF16)
    lo = (r1 - mid.astype(F32)).astype(BF16)
    return [hi, mid, lo]


def _rms(x, g):
    return x * lax.rsqrt(jnp.mean(x * x, axis=-1, keepdims=True) + RMS_EPS) * g


def _silu(x):
    return x * jax.nn.sigmoid(x)


def _const_spec(shape):
    nd = len(shape)
    return pl.BlockSpec(shape, lambda *_: (0,) * nd, pipeline_mode=pl.Buffered(1))


def _col_block_spec(rows, width, block):
    return pl.BlockSpec((rows, width), lambda *_: (0, block), pipeline_mode=pl.Buffered(1))


def _params(sem):
    return pltpu.CompilerParams(dimension_semantics=sem, vmem_limit_bytes=VMEM_LIMIT_BYTES)


def _ffn_kernel(x_ref, gpre_ref, wg_ref, wu_ref, wd_ref, gpost_ref, *refs, parts):
    n_side = (len(refs) - 1) // 2
    o_ref = refs[n_side]
    for src, dst in zip(refs[:n_side], refs[n_side + 1:]):
        dst[...] = src[...].astype(BF16)
    rows = x_ref.shape[0] // parts
    sls = [slice(i * rows, (i + 1) * rows) for i in range(parts)]
    hs = [_rms(x_ref[sl, :], gpre_ref[...]).astype(BF16) for sl in sls]
    ys = []
    for i, sl in enumerate(sls):
        gate = jnp.dot(hs[i], wg_ref[...], preferred_element_type=F32)
        up = jnp.dot(hs[i], wu_ref[...], preferred_element_type=F32)
        a = (_silu(gate) * up).astype(BF16)
        ys.append(jnp.dot(a, wd_ref[...], preferred_element_type=F32))
        if i > 0:
            o_ref[sls[i - 1], :] = x_ref[sls[i - 1], :] + 0.5 * _rms(ys[i - 1], gpost_ref[...])
    o_ref[sls[-1], :] = x_ref[sls[-1], :] + 0.5 * _rms(ys[-1], gpost_ref[...])


def _ffn(x2, gpre, wg, wu, wd, gpost, side=(), tm=1024, parts=4):
    T, D = x2.shape
    F = wg.shape[-1]
    n = T // tm
    side_in, side_out, side_shape = [], [], []
    for arr, lead in side:
        R, Cc = arr.shape[-2:]
        div = 1 if R % (BF16_SUBLANES * n) == 0 else 2
        rb = R * div // n
        assert R % rb == 0 and rb % BF16_SUBLANES == 0, (arr.shape, n)
        side_in.append(pl.BlockSpec((None,) * len(lead) + (rb, Cc),
                                    lambda i, lead=lead, div=div: lead + (i // div, 0)))
        side_out.append(pl.BlockSpec((rb, Cc), lambda i, div=div: (i // div, 0)))
        side_shape.append(jax.ShapeDtypeStruct((R, Cc), BF16))
    out = pl.pallas_call(
        functools.partial(_ffn_kernel, parts=parts),
        grid=(n,),
        in_specs=[pl.BlockSpec((tm, D), lambda i: (i, 0)),
                  _const_spec((1, D)), _const_spec((D, F)), _const_spec((D, F)),
                  _const_spec((F, D)), _const_spec((1, D))] + side_in,
        out_specs=[pl.BlockSpec((tm, D), lambda i: (i, 0))] + side_out,
        out_shape=[jax.ShapeDtypeStruct((T, D), F32)] + side_shape,
        compiler_params=_params(("arbitrary",)),
        name="ffn",
    )(x2, gpre, wg, wu, wd, gpost, *[arr for arr, _ in side])
    return out[0], out[1:]


def _dn_in_kernel(x_ref, gpre_ref, wqkv_ref, wba_ref, wbat_ref, conv_ref, alog_c_ref, dtb_c_ref,
                  alog_r_ref, dtb_r_ref, perm_ref, cum_ref, cumt_ref,
                  q_ref, k_ref, v_ref, gcol_ref, grow_ref, *carry_refs, tm):
    H, Dh, C, P = DN_HEADS, DN_HEAD_DIM, DN_CHUNK, DN_CONV
    bw = 2 * Dh
    nb = H * Dh // bw
    Q = tm // P

    @pl.when(pl.program_id(1) == 0)
    def _():
        for carry in carry_refs:
            carry[...] = jnp.zeros(carry.shape, F32)

    x = x_ref[0]
    h_nat = _rms(x, gpre_ref[...]).astype(BF16)
    h = jnp.dot(perm_ref[...], h_nat, preferred_element_type=F32).astype(BF16)
    n_blocks = len(carry_refs)

    def project(cb):
        cols = slice(cb * bw, (cb + 1) * bw)
        return jnp.dot(h, wqkv_ref[:, cols], preferred_element_type=F32)

    ba = jnp.dot(h_nat, wba_ref[...], preferred_element_type=F32)
    bat = lax.dot_general(wbat_ref[...], h_nat, (((1,), (1,)), ((), ())),
                          preferred_element_type=F32)
    projs = {0: project(0), 1: project(1)}
    beta_c = jax.nn.sigmoid(ba[:, :H])
    g_c = -jnp.exp(alog_c_ref[...]) * jax.nn.softplus(ba[:, H:] + dtb_c_ref[...])
    g_r = -jnp.exp(alog_r_ref[...]) * jax.nn.softplus(bat[H:, :] + dtb_r_ref[...])
    gc_c = sum(jnp.dot(cum_ref[...], t, preferred_element_type=F32) for t in _bf16_terms(g_c))
    gc_r3 = jnp.dot(jnp.concatenate(_bf16_terms(g_r), axis=0), cumt_ref[...],
                    preferred_element_type=F32)
    gc_r = gc_r3[0:H] + gc_r3[H:2 * H] + gc_r3[2 * H:3 * H]
    gcol_ref[0, :, 0:H] = beta_c
    gcol_ref[0, :, H:2 * H] = gc_c
    for c in range(tm // C):
        grow_ref[0, c] = gc_r[:, c * C:(c + 1) * C]

    for cb, carry in enumerate(carry_refs):
        cols = slice(cb * bw, (cb + 1) * bw)
        proj = projs.pop(cb)
        cw = conv_ref[:, cols]
        phase = [proj[p * Q:(p + 1) * Q] for p in range(P)]
        up = {p: jnp.concatenate([carry[p - 1], phase[p]], axis=0)[7:7 + Q] for p in range(1, P)}
        for p in range(1, P):
            carry[p - 1] = phase[p][Q - 8:Q]
        ys = []
        for p in range(P):
            acc = phase[p] * cw[P - 1:P, :]
            for s in range(1, P):
                src = phase[p - s] if p >= s else up[p - s + P]
                acc = acc + src * cw[P - 1 - s:P - s, :]
            ys.append(_silu(acc))
        if cb + 2 < n_blocks:
            projs[cb + 2] = project(cb + 2)
        out_ref = (q_ref, k_ref, v_ref)[cb // nb]
        for i in range(bw // Dh):
            hd = (cb % nb) * (bw // Dh) + i
            for p in range(P):
                yh = ys[p][:, i * Dh:(i + 1) * Dh]
                if cb < 2 * nb:
                    scale = Dh ** -0.5 if cb < nb else 1.0
                    yh = yh * (lax.rsqrt(jnp.sum(yh * yh, axis=-1, keepdims=True) + L2_EPS) * scale)
                out_ref[0, hd, pl.ds(p, Q, stride=P), :] = yh


def _dn_in(x, gpre, wqkv, wba, wbat, conv_w, alog, dtb, tm=256):
    B, S, D = x.shape
    H, Dh, C, P = DN_HEADS, DN_HEAD_DIM, DN_CHUNK, DN_CONV
    W = H * Dh
    bw = 2 * Dh
    Q = tm // P
    nat = jnp.arange(tm)
    perm = ((P * (nat % Q) + nat // Q)[:, None] == nat[None, :]).astype(BF16)
    cum = ((nat[:, None] // C == nat[None, :] // C) & (nat[None, :] <= nat[:, None])).astype(BF16)
    head_major = lambda b, j: (b, 0, j, 0)
    return pl.pallas_call(
        functools.partial(_dn_in_kernel, tm=tm),
        grid=(B, S // tm),
        in_specs=[pl.BlockSpec((1, tm, D), lambda b, j: (b, j, 0)),
                  _const_spec((1, D)), _col_block_spec(D, 3 * W, 0), _const_spec((D, 2 * H)),
                  _const_spec((2 * H, D)), _const_spec((P, 3 * W)),
                  _const_spec((1, H)), _const_spec((1, H)), _const_spec((H, 1)), _const_spec((H, 1)),
                  _const_spec((tm, tm)), _const_spec((tm, tm)), _const_spec((tm, tm))],
        out_specs=[pl.BlockSpec((1, H, tm, Dh), head_major), pl.BlockSpec((1, H, tm, Dh), head_major),
                   pl.BlockSpec((1, H, tm, Dh), head_major),
                   pl.BlockSpec((1, tm, 2 * H), lambda b, j: (b, j, 0)),
                   pl.BlockSpec((1, tm // C, H, C), lambda b, j: (b, j, 0, 0))],
        out_shape=[jax.ShapeDtypeStruct((B, H, S, Dh), F32), jax.ShapeDtypeStruct((B, H, S, Dh), F32),
                   jax.ShapeDtypeStruct((B, H, S, Dh), F32), jax.ShapeDtypeStruct((B, S, 2 * H), F32),
                   jax.ShapeDtypeStruct((B, S // C, H, C), F32)],
        scratch_shapes=[pltpu.VMEM((P - 1, 8, bw), F32) for _ in range(3 * W // bw)],
        compiler_params=_params(("arbitrary", "arbitrary")),
        name="dn_in",
    )(x, gpre, wqkv, wba, wbat, conv_w, alog.reshape(1, H), dtb.reshape(1, H),
      alog.reshape(H, 1), dtb.reshape(H, 1), perm, cum, cum.T)


def _bmm(a, b):
    return lax.dot_general(a.astype(BF16), b.astype(BF16), (((2,), (1,)), ((0,), (0,))),
                           preferred_element_type=F32)


def _bmm_nt(a, b):
    return lax.dot_general(a.astype(BF16), b.astype(BF16), (((2,), (2,)), ((0,), (0,))),
                           preferred_element_type=F32)


def _bmm_tn(a, b):
    return lax.dot_general(a.astype(BF16), b.astype(BF16), (((1,), (1,)), ((0,), (0,))),
                           preferred_element_type=F32)


def _unit_lower_inverse_steps(low, ri, ci):
    C = low.shape[-1]
    eye = jnp.where(ri == ci, 1.0, 0.0).astype(F32)
    t = eye - jnp.where((ri // 2) == (ci // 2), low, 0.0)
    b = 2
    while b < C:
        off = jnp.where(((ri // (2 * b)) == (ci // (2 * b))) & ((ri // b) != (ci // b)), low, 0.0)
        if b < SUBLANES:
            t = t - _bmm(t, _bmm(off, t))
        else:
            lower = [slice(s, s + b) for s in range(b, C, 2 * b)]
            rows = lambda a: jnp.concatenate([a[:, r] for r in lower], axis=1)
            x_lo = _bmm(rows(off), t)
            zero = jnp.zeros_like(x_lo[:, :b])
            x = jnp.concatenate([p for i in range(len(lower))
                                 for p in (zero, x_lo[:, i * b:(i + 1) * b])], axis=1)
            corr = _bmm(rows(t), x)
            t = jnp.concatenate([p for i, r in enumerate(lower)
                                 for p in (t[:, r.start - b:r.start],
                                           t[:, r] - corr[:, i * b:(i + 1) * b])], axis=1)
        b *= 2
        yield None
    yield t


def _dn_delta_kernel(q_ref, k_ref, v_ref, gcol_ref, grow_ref, spread_ref, o_ref, s_ref, *, tt):
    H, Dh, C = DN_HEADS, DN_HEAD_DIM, DN_CHUNK
    nc = tt // C

    @pl.when(pl.program_id(1) == 0)
    def _():
        s_ref[...] = jnp.zeros_like(s_ref)

    ri = lax.broadcasted_iota(jnp.int32, (1, C, C), 1)
    ci = lax.broadcasted_iota(jnp.int32, (1, C, C), 2)
    causal = ci <= ri
    strict = ci < ri

    gates = gcol_ref[0]
    spread = jnp.dot(jnp.concatenate(_bf16_terms(gates), axis=1), spread_ref[...],
                     preferred_element_type=F32)

    def local_steps(chunks):
        per_head = lambda ref: jnp.stack([ref[0, hd, c * C:(c + 1) * C, :]
                                          for c in chunks for hd in range(H)])
        q, k, v = per_head(q_ref), per_head(k_ref), per_head(v_ref)
        beta = jnp.stack([spread[c * C:(c + 1) * C, hd * Dh:(hd + 1) * Dh]
                          for c in chunks for hd in range(H)])
        gc = jnp.stack([spread[c * C:(c + 1) * C, (H + hd) * Dh:(H + hd + 1) * Dh]
                        for c in chunks for hd in range(H)])
        gr = jnp.stack([grow_ref[0, c, hd:hd + 1, :] for c in chunks for hd in range(H)])
        gl = gc[:, C - 1:C, :]
        decay = jnp.where(causal, jnp.exp(jnp.where(causal, gc[:, :, :C] - gr, 0.0)), 0.0)
        egc = jnp.exp(gc)
        kb = k * beta
        kq = _bmm_nt(jnp.concatenate([kb, q], axis=1), k)
        yield None
        low = jnp.where(strict, kq[:, :C] * decay, 0.0)
        attn = kq[:, C:] * decay
        t = None
        for t in _unit_lower_inverse_steps(low, ri, ci):
            if t is None:
                yield None
        uw = _bmm(t, jnp.concatenate([v * beta, kb * egc], axis=2))
        wq = jnp.concatenate([uw[:, :, Dh:], q * egc], axis=1).astype(BF16)
        kdt = jnp.swapaxes(k * jnp.exp(gl - gc), 1, 2)
        akd = jnp.concatenate([attn, kdt], axis=1).astype(BF16)
        yield dict(u=uw[:, :, :Dh], wq=wq, akd=akd, egl=jnp.exp(gl))

    def recurrent_steps(chunks, loc, state):
        for i, c in enumerate(chunks):
            sl = slice(i * H, (i + 1) * H)
            ws = _bmm(loc["wq"][sl], state)
            v_new = loc["u"][sl] - ws[:, :C]
            yield None
            av = _bmm(loc["akd"][sl], v_new)
            o = ws[:, C:] + av[:, :C]
            for hd in range(H):
                o_ref[0, c * C:(c + 1) * C, hd * Dh:(hd + 1) * Dh] = o[hd]
            state = state * loc["egl"][sl] + av[:, C:]
            yield None
        yield state

    groups = [list(range(g, min(g + DN_GROUP_CHUNKS, nc))) for g in range(0, nc, DN_GROUP_CHUNKS)]
    state = s_ref[...]
    loc_prev = None
    for gi in range(len(groups) + 1):
        rec = recurrent_steps(groups[gi - 1], loc_prev, state) if gi > 0 else iter(())
        loc = None
        if gi < len(groups):
            for loc in local_steps(groups[gi]):
                out = next(rec, None)
                if out is not None:
                    state = out
        for out in rec:
            if out is not None:
                state = out
        loc_prev = loc
    s_ref[...] = state


def _dn_delta(q, k, v, gcol, grow, tt=1024):
    B, H, S, Dh = q.shape
    C = DN_CHUNK
    W = H * Dh
    tok = lambda b, j: (b, j, 0)
    head_major = pl.BlockSpec((1, H, tt, Dh), lambda b, j: (b, 0, j, 0))
    spread = (jnp.arange(2 * H)[:, None] == jnp.arange(2 * W)[None, :] // Dh).astype(BF16)
    spread = jnp.concatenate([spread] * 3, axis=0)
    return pl.pallas_call(
        functools.partial(_dn_delta_kernel, tt=tt),
        grid=(B, S // tt),
        in_specs=[head_major, head_major, head_major, pl.BlockSpec((1, tt, 2 * H), tok),
                  pl.BlockSpec((1, tt // C, H, C), lambda b, j: (b, j, 0, 0)),
                  _const_spec((3 * 2 * H, 2 * W))],
        out_specs=pl.BlockSpec((1, tt, W), tok),
        out_shape=jax.ShapeDtypeStruct((B, S, W), F32),
        scratch_shapes=[pltpu.VMEM((H, Dh, Dh), F32)],
        compiler_params=_params(("arbitrary", "arbitrary")),
        name="dn_delta",
    )(q, k, v, gcol, grow, spread)


def _dn_out_kernel(x_ref, o_ref, gpre_ref, wz_ref, ng_ref, wout_ref, gpost_ref, y_ref, *, halves):
    H, Dh = DN_HEADS, DN_HEAD_DIM
    rows = x_ref.shape[0] // halves
    sls = [slice(i * rows, (i + 1) * rows) for i in range(halves)]
    ng = ng_ref[...]
    hs = [_rms(x_ref[sl, :], gpre_ref[...]).astype(BF16) for sl in sls]
    zs = [jnp.dot(h, wz_ref[...], preferred_element_type=F32) for h in hs]
    for i, sl in enumerate(sls):
        parts = []
        for hd in range(H):
            oh = o_ref[sl, hd * Dh:(hd + 1) * Dh]
            parts.append((_rms(oh, ng) * _silu(zs[i][:, hd * Dh:(hd + 1) * Dh])).astype(BF16))
        m = jnp.dot(jnp.concatenate(parts, axis=-1), wout_ref[...], preferred_element_type=F32)
        y_ref[sl, :] = x_ref[sl, :] + _rms(m, gpost_ref[...])


def _dn_out(x2, o2, gpre, w_in, z_block, ng, wout, gpost, tm=1024, halves=2):
    T, D = x2.shape
    W = o2.shape[1]
    row = lambda i: (i, 0)
    return pl.pallas_call(
        functools.partial(_dn_out_kernel, halves=halves),
        grid=(T // tm,),
        in_specs=[pl.BlockSpec((tm, D), row), pl.BlockSpec((tm, W), row),
                  _const_spec((1, D)), _col_block_spec(D, W, z_block), _const_spec((1, DN_HEAD_DIM)),
                  _const_spec((W, D)), _const_spec((1, D))],
        out_specs=pl.BlockSpec((tm, D), row),
        out_shape=jax.ShapeDtypeStruct((T, D), F32),
        compiler_params=_params(("parallel",)),
        name="dn_out",
    )(x2, o2, gpre, w_in, ng, wout, gpost)


def _sg_kernel(x_ref, gpre_ref, win_ref, bin_ref, lng_ref, lnb_ref, ws_ref, bst_ref, wout_ref,
               gpost_ref, y_ref, *, tm):
    G, C = SG_GROUPS, SG_CHUNK
    E = win_ref.shape[1] // 2
    Eg = E // G
    x = x_ref[...]
    h = _rms(x, gpre_ref[...]).astype(BF16)

    def in_proj(col0):
        cols = slice(col0, col0 + Eg)
        z = jnp.dot(h, win_ref[:, cols], preferred_element_type=F32) + bin_ref[:, cols]
        return 0.5 * z * (1.0 + lax.erf(z * (2.0 ** -0.5)))

    vs = [in_proj(E + g * Eg) for g in range(G)]
    us = [in_proj(g * Eg) for g in range(G)]
    tot = vs[0]
    for g in range(1, G):
        tot = tot + vs[g]
    mu = jnp.sum(tot, axis=-1, keepdims=True) * (1.0 / E)
    xcs = [v - mu for v in vs]
    tot = xcs[0] * xcs[0]
    for g in range(1, G):
        tot = tot + xcs[g] * xcs[g]
    rstd = lax.rsqrt(jnp.sum(tot, axis=-1, keepdims=True) * (1.0 / E) + LN_EPS)

    ri = lax.broadcasted_iota(jnp.int32, (C, C), 0)
    ci = lax.broadcasted_iota(jnp.int32, (C, C), 1)
    bst = bst_ref[...]
    parts = []
    for g in range(G):
        cols = slice(g * Eg, (g + 1) * Eg)
        vn = (xcs[g] * rstd * lng_ref[:, cols] + lnb_ref[:, cols]).astype(BF16)
        wc = jnp.where(ci <= ri, ws_ref[g], 0.0).astype(BF16)
        rows = []
        for c in range(tm // C):
            mixed = jnp.dot(wc, vn[c * C:(c + 1) * C], preferred_element_type=F32) + bst[:, g:g + 1]
            rows.append((us[g][c * C:(c + 1) * C] * mixed).astype(BF16))
        parts.append(jnp.concatenate(rows, axis=0))
    gated = jnp.concatenate(parts, axis=-1)
    m = jnp.dot(gated, wout_ref[...], preferred_element_type=F32)
    y_ref[...] = x + _rms(m, gpost_ref[...])


def _sg(x2, gpre, win, b_in, lng, lnb, ws, bst, wout, gpost, tm=1024):
    T, D = x2.shape
    E2 = win.shape[1]
    E = E2 // 2
    G, C = SG_GROUPS, SG_CHUNK
    row = lambda i: (i, 0)
    return pl.pallas_call(
        functools.partial(_sg_kernel, tm=tm),
        grid=(T // tm,),
        in_specs=[pl.BlockSpec((tm, D), row), _const_spec((1, D)), _const_spec((D, E2)),
                  _const_spec((1, E2)), _const_spec((1, E)), _const_spec((1, E)),
                  _const_spec((G, C, C)), _const_spec((C, G)), _const_spec((E, D)),
                  _const_spec((1, D))],
        out_specs=pl.BlockSpec((tm, D), row),
        out_shape=jax.ShapeDtypeStruct((T, D), F32),
        compiler_params=_params(("parallel",)),
        name="sg",
    )(x2, gpre, win, b_in, lng, lnb, ws, bst, wout, gpost)


def kernel(x, norm_g, ffn_w_gate, ffn_w_up, ffn_w_down, dn_w_in, dn_conv_w, dn_a_log, dn_dt_bias,
           dn_norm_g, dn_w_out, sg_w_in, sg_b_in, sg_ln_g, sg_ln_b, sg_w_s, sg_b_s, sg_w_out):
    B, S, D = x.shape
    depth = norm_g.shape[0]
    H = DN_HEADS
    W = H * DN_HEAD_DIM
    ffn_f32 = lambda i, k: [(w, (i, k)) for w in (ffn_w_gate, ffn_w_up, ffn_w_down)]
    ffn_w = [w[0, 0].astype(BF16) for w in (ffn_w_gate, ffn_w_up, ffn_w_down)]
    x2 = x.reshape(B * S, D)
    for i in range(depth):
        ng = norm_g[i].reshape(6, 1, D)
        j = i // 2
        mixer_f32 = ([(dn_w_in, (j,)), (dn_w_out, (j,))] if i % 2 == 0
                     else [(sg_w_in, (j,)), (sg_w_out, (j,))])
        x2, cast = _ffn(x2, ng[0], *ffn_w, ng[1], side=mixer_f32 + ffn_f32(i, 1))
        w_in, w_out, ffn_w = cast[0], cast[1], cast[2:]
        if i % 2 == 0:
            q, k, v, gcol, grow = _dn_in(x2.reshape(B, S, D), ng[2], w_in,
                                         w_in[:, 4 * W:], w_in[:, 4 * W:].T, dn_conv_w[j],
                                         dn_a_log[j], dn_dt_bias[j])
            o = _dn_delta(q, k, v, gcol, grow)
            x2 = _dn_out(x2, o.reshape(B * S, W), ng[2], w_in, 3,
                         dn_norm_g[j].reshape(1, -1), w_out, ng[3])
        else:
            E = sg_w_in.shape[2] // 2
            x2 = _sg(x2, ng[2], w_in, sg_b_in[j].reshape(1, -1),
                     sg_ln_g[j].reshape(1, E), sg_ln_b[j].reshape(1, E), sg_w_s[j], sg_b_s[j].T,
                     w_out, ng[3])
        x2, ffn_w = _ffn(x2, ng[4], *ffn_w, ng[5], side=ffn_f32(i + 1, 0) if i + 1 < depth else ())
    return x2.reshape(B, S, D)
```

```python
import functools

import jax
import jax.numpy as jnp
from jax import lax
from jax.experimental import pallas as pl
from jax.experimental.pallas import tpu as pltpu

RMS_EPS = 1e-6
LN_EPS = 1e-5
L2_EPS = 1e-6
DN_HEADS = 8
DN_HEAD_DIM = 128
DN_CONV = 4
DN_CHUNK = 64
SUBLANES = 8
DN_GROUP_CHUNKS = 4
SG_GROUPS = 8
SG_CHUNK = 128

VMEM_LIMIT_BYTES = 56 * 1024 * 1024

BF16 = jnp.bfloat16
F32 = jnp.float32


def _bf16_terms(x):
    hi = x.astype(BF16)
    r1 = x - hi.astype(F32)
    mid = r1.astype(BF16)
    lo = (r1 - mid.astype(F32)).astype(BF16)
    return [hi, mid, lo]


def _rms(x, g):
    return x * lax.rsqrt(jnp.mean(x * x, axis=-1, keepdims=True) + RMS_EPS) * g


def _silu(x):
    return x * jax.nn.sigmoid(x)


def _const_spec(shape):
    nd = len(shape)
    return pl.BlockSpec(shape, lambda *_: (0,) * nd, pipeline_mode=pl.Buffered(1))


def _col_block_spec(rows, width, block):
    return pl.BlockSpec((rows, width), lambda *_: (0, block), pipeline_mode=pl.Buffered(1))


def _params(sem):
    return pltpu.CompilerParams(dimension_semantics=sem, vmem_limit_bytes=VMEM_LIMIT_BYTES)


def _ffn_kernel(x_ref, gpre_ref, wg_ref, wu_ref, wd_ref, gpost_ref, o_ref, *, parts):
    rows = x_ref.shape[0] // parts
    sls = [slice(i * rows, (i + 1) * rows) for i in range(parts)]
    hs = [_rms(x_ref[sl, :], gpre_ref[...]).astype(BF16) for sl in sls]
    ys = []
    for i, sl in enumerate(sls):
        gate = jnp.dot(hs[i], wg_ref[...], preferred_element_type=F32)
        up = jnp.dot(hs[i], wu_ref[...], preferred_element_type=F32)
        a = (_silu(gate) * up).astype(BF16)
        ys.append(jnp.dot(a, wd_ref[...], preferred_element_type=F32))
        if i > 0:
            o_ref[sls[i - 1], :] = x_ref[sls[i - 1], :] + 0.5 * _rms(ys[i - 1], gpost_ref[...])
    o_ref[sls[-1], :] = x_ref[sls[-1], :] + 0.5 * _rms(ys[-1], gpost_ref[...])


def _ffn(x2, gpre, wg, wu, wd, gpost, layer, slot, tm=1024, parts=4):
    T, D = x2.shape
    F = wg.shape[-1]
    pick = lambda r, c: pl.BlockSpec((None, None, r, c), lambda *_: (layer, slot, 0, 0),
                                     pipeline_mode=pl.Buffered(1))
    return pl.pallas_call(
        functools.partial(_ffn_kernel, parts=parts),
        grid=(T // tm,),
        in_specs=[pl.BlockSpec((tm, D), lambda i: (i, 0)),
                  _const_spec((1, D)), pick(D, F), pick(D, F), pick(F, D), _const_spec((1, D))],
        out_specs=pl.BlockSpec((tm, D), lambda i: (i, 0)),
        out_shape=jax.ShapeDtypeStruct((T, D), F32),
        compiler_params=_params(("parallel",)),
        name="ffn",
    )(x2, gpre, wg, wu, wd, gpost)


def _dn_in_kernel(x_ref, gpre_ref, wqkv_ref, wba_ref, wbat_ref, conv_ref, alog_c_ref, dtb_c_ref,
                  alog_r_ref, dtb_r_ref, perm_ref, cum_ref, cumt_ref, ones_ref,
                  q_ref, k_ref, v_ref, gcol_ref, grow_ref, *carry_refs, tm):
    H, Dh, C, P = DN_HEADS, DN_HEAD_DIM, DN_CHUNK, DN_CONV
    bw = 2 * Dh
    nb = H * Dh // bw
    Q = tm // P

    @pl.when(pl.program_id(1) == 0)
    def _():
        for carry in carry_refs:
            carry[...] = jnp.zeros(carry.shape, F32)

    x = x_ref[0]
    h_nat = _rms(x, gpre_ref[...]).astype(BF16)
    h = jnp.dot(perm_ref[...], h_nat, preferred_element_type=F32).astype(BF16)
    n_blocks = len(carry_refs)

    def project(cb):
        cols = slice(cb * bw, (cb + 1) * bw)
        return jnp.dot(h, wqkv_ref[:, cols], preferred_element_type=F32)

    ba = jnp.dot(h_nat, wba_ref[...], preferred_element_type=F32)
    bat = lax.dot_general(wbat_ref[...], h_nat, (((1,), (1,)), ((), ())),
                          preferred_element_type=F32)
    projs = {0: project(0), 1: project(1), 2: project(2)}
    beta_c = jax.nn.sigmoid(ba[:, :H])
    g_c = -jnp.exp(alog_c_ref[...]) * jax.nn.softplus(ba[:, H:] + dtb_c_ref[...])
    g_r = -jnp.exp(alog_r_ref[...]) * jax.nn.softplus(bat[H:, :] + dtb_r_ref[...])
    gc_c = sum(jnp.dot(cum_ref[...], t, preferred_element_type=F32) for t in _bf16_terms(g_c))
    gc_r3 = jnp.dot(jnp.concatenate(_bf16_terms(g_r), axis=0), cumt_ref[...],
                    preferred_element_type=F32)
    gc_r = gc_r3[0:H] + gc_r3[H:2 * H] + gc_r3[2 * H:3 * H]
    gcol_ref[0, :, 0:H] = beta_c
    gcol_ref[0, :, H:2 * H] = gc_c
    for c in range(tm // C):
        grow_ref[0, c] = gc_r[:, c * C:(c + 1) * C]

    for cb, carry in enumerate(carry_refs):
        cols = slice(cb * bw, (cb + 1) * bw)
        proj = projs.pop(cb)
        cw = conv_ref[:, cols]
        phase = [proj[p * Q:(p + 1) * Q] for p in range(P)]
        up = {p: jnp.concatenate([carry[p - 1], phase[p]], axis=0)[7:7 + Q] for p in range(1, P)}
        for p in range(1, P):
            carry[p - 1] = phase[p][Q - 8:Q]
        ys = []
        for p in range(P):
            acc = phase[p] * cw[P - 1:P, :]
            for s in range(1, P):
                src = phase[p - s] if p >= s else up[p - s + P]
                acc = acc + src * cw[P - 1 - s:P - s, :]
            ys.append(_silu(acc))
        if cb + 3 < n_blocks:
            projs[cb + 3] = project(cb + 3)
        out_ref = (q_ref, k_ref, v_ref)[cb // nb]
        y = jnp.concatenate(ys, axis=0)
        if cb < 2 * nb:
            scale = Dh ** -0.5 if cb < nb else 1.0
            ss = jnp.dot((y * y).astype(BF16), ones_ref[...], preferred_element_type=F32)
            y = y * (lax.rsqrt(ss + L2_EPS) * scale)
        for i in range(bw // Dh):
            hd = (cb % nb) * (bw // Dh) + i
            for p in range(P):
                out_ref[0, hd, pl.ds(p, Q, stride=P), :] = y[p * Q:(p + 1) * Q, i * Dh:(i + 1) * Dh]


def _dn_in(x, gpre, wqkv, wba, wbat, conv_w, alog, dtb, tm=256):
    B, S, D = x.shape
    H, Dh, C, P = DN_HEADS, DN_HEAD_DIM, DN_CHUNK, DN_CONV
    W = H * Dh
    bw = 2 * Dh
    Q = tm // P
    nat = jnp.arange(tm)
    perm = ((P * (nat % Q) + nat // Q)[:, None] == nat[None, :]).astype(BF16)
    cum = ((nat[:, None] // C == nat[None, :] // C) & (nat[None, :] <= nat[:, None])).astype(BF16)
    head_ones = (jnp.arange(bw)[:, None] // Dh == jnp.arange(bw)[None, :] // Dh).astype(BF16)
    head_major = lambda b, j: (b, 0, j, 0)
    return pl.pallas_call(
        functools.partial(_dn_in_kernel, tm=tm),
        grid=(B, S // tm),
        in_specs=[pl.BlockSpec((1, tm, D), lambda b, j: (b, j, 0)),
                  _const_spec((1, D)), _col_block_spec(D, 3 * W, 0), _const_spec((D, 2 * H)),
                  _const_spec((2 * H, D)), _const_spec((P, 3 * W)),
                  _const_spec((1, H)), _const_spec((1, H)), _const_spec((H, 1)), _const_spec((H, 1)),
                  _const_spec((tm, tm)), _const_spec((tm, tm)), _const_spec((tm, tm)),
                  _const_spec((bw, bw))],
        out_specs=[pl.BlockSpec((1, H, tm, Dh), head_major), pl.BlockSpec((1, H, tm, Dh), head_major),
                   pl.BlockSpec((1, H, tm, Dh), head_major),
                   pl.BlockSpec((1, tm, 2 * H), lambda b, j: (b, j, 0)),
                   pl.BlockSpec((1, tm // C, H, C), lambda b, j: (b, j, 0, 0))],
        out_shape=[jax.ShapeDtypeStruct((B, H, S, Dh), F32), jax.ShapeDtypeStruct((B, H, S, Dh), F32),
                   jax.ShapeDtypeStruct((B, H, S, Dh), F32), jax.ShapeDtypeStruct((B, S, 2 * H), F32),
                   jax.ShapeDtypeStruct((B, S // C, H, C), F32)],
        scratch_shapes=[pltpu.VMEM((P - 1, 8, bw), F32) for _ in range(3 * W // bw)],
        compiler_params=_params(("arbitrary", "arbitrary")),
        name="dn_in",
    )(x, gpre, wqkv, wba, wbat, conv_w, alog.reshape(1, H), dtb.reshape(1, H),
      alog.reshape(H, 1), dtb.reshape(H, 1), perm, cum, cum.T, head_ones)


def _bmm(a, b):
    return lax.dot_general(a.astype(BF16), b.astype(BF16), (((2,), (1,)), ((0,), (0,))),
                           preferred_element_type=F32)


def _bmm_nt(a, b):
    return lax.dot_general(a.astype(BF16), b.astype(BF16), (((2,), (2,)), ((0,), (0,))),
                           preferred_element_type=F32)


def _unit_lower_inverse_steps(low, ri, ci):
    C = low.shape[-1]
    eye = jnp.where(ri == ci, 1.0, 0.0).astype(F32)
    t = eye - jnp.where((ri // 2) == (ci // 2), low, 0.0)
    b = 2
    while b < C:
        off = jnp.where(((ri // (2 * b)) == (ci // (2 * b))) & ((ri // b) != (ci // b)), low, 0.0)
        if b < SUBLANES:
            t = t - _bmm(t, _bmm(off, t))
        else:
            lower = [slice(s, s + b) for s in range(b, C, 2 * b)]
            rows = lambda a: jnp.concatenate([a[:, r] for r in lower], axis=1)
            x_lo = _bmm(rows(off), t)
            zero = jnp.zeros_like(x_lo[:, :b])
            x = jnp.concatenate([p for i in range(len(lower))
                                 for p in (zero, x_lo[:, i * b:(i + 1) * b])], axis=1)
            corr = _bmm(rows(t), x)
            t = jnp.concatenate([p for i, r in enumerate(lower)
                                 for p in (t[:, r.start - b:r.start],
                                           t[:, r] - corr[:, i * b:(i + 1) * b])], axis=1)
        b *= 2
        yield None
    yield t


def _dn_delta_kernel(q_ref, k_ref, v_ref, gcol_ref, grow_ref, spread_ref, o_ref, s_ref, *, tt):
    H, Dh, C = DN_HEADS, DN_HEAD_DIM, DN_CHUNK
    nc = tt // C

    @pl.when(pl.program_id(1) == 0)
    def _():
        s_ref[...] = jnp.zeros_like(s_ref)

    ri = lax.broadcasted_iota(jnp.int32, (1, C, C), 1)
    ci = lax.broadcasted_iota(jnp.int32, (1, C, C), 2)
    causal = ci <= ri
    strict = ci < ri

    gates = gcol_ref[0]
    spread = jnp.dot(jnp.concatenate(_bf16_terms(gates), axis=1), spread_ref[...],
                     preferred_element_type=F32)

    def local_steps(chunks):
        per_head = lambda ref: jnp.stack([ref[0, hd, c * C:(c + 1) * C, :]
                                          for c in chunks for hd in range(H)])
        q, k, v = per_head(q_ref), per_head(k_ref), per_head(v_ref)
        beta = jnp.stack([spread[c * C:(c + 1) * C, hd * Dh:(hd + 1) * Dh]
                          for c in chunks for hd in range(H)])
        gc = jnp.stack([spread[c * C:(c + 1) * C, (H + hd) * Dh:(H + hd + 1) * Dh]
                        for c in chunks for hd in range(H)])
        gr = jnp.stack([grow_ref[0, c, hd:hd + 1, :] for c in chunks for hd in range(H)])
        gl = gc[:, C - 1:C, :]
        decay = jnp.where(causal, jnp.exp(jnp.where(causal, gc[:, :, :C] - gr, 0.0)), 0.0)
        egc = jnp.exp(gc)
        kb = k * beta
        kq = _bmm_nt(jnp.concatenate([kb, q], axis=1), k)
        yield None
        low = jnp.where(strict, kq[:, :C] * decay, 0.0)
        attn = kq[:, C:] * decay
        t = None
        for t in _unit_lower_inverse_steps(low, ri, ci):
            if t is None:
                yield None
        uw = _bmm(t, jnp.concatenate([v * beta, kb * egc], axis=2))
        wq = jnp.concatenate([uw[:, :, Dh:], q * egc], axis=1).astype(BF16)
        kdt = jnp.swapaxes(k * jnp.exp(gl - gc), 1, 2)
        akd = jnp.concatenate([attn, kdt], axis=1).astype(BF16)
        yield dict(u=uw[:, :, :Dh], wq=wq, akd=akd, egl=jnp.exp(gl))

    def recurrent_steps(chunks, loc, state):
        for i, c in enumerate(chunks):
            sl = slice(i * H, (i + 1) * H)
            ws = _bmm(loc["wq"][sl], state)
            v_new = loc["u"][sl] - ws[:, :C]
            yield None
            av = _bmm(loc["akd"][sl], v_new)
            o = ws[:, C:] + av[:, :C]
            for hd in range(H):
                o_ref[0, c * C:(c + 1) * C, hd * Dh:(hd + 1) * Dh] = o[hd]
            state = state * loc["egl"][sl] + av[:, C:]
            yield None
        yield state

    groups = [list(range(g, min(g + DN_GROUP_CHUNKS, nc))) for g in range(0, nc, DN_GROUP_CHUNKS)]
    state = s_ref[...]
    loc_prev = None
    for gi in range(len(groups) + 1):
        rec = recurrent_steps(groups[gi - 1], loc_prev, state) if gi > 0 else iter(())
        loc = None
        if gi < len(groups):
            for loc in local_steps(groups[gi]):
                out = next(rec, None)
                if out is not None:
                    state = out
        for out in rec:
            if out is not None:
                state = out
        loc_prev = loc
    s_ref[...] = state


def _dn_delta(q, k, v, gcol, grow, tt=1024):
    B, H, S, Dh = q.shape
    C = DN_CHUNK
    W = H * Dh
    tok = lambda b, j: (b, j, 0)
    head_major = pl.BlockSpec((1, H, tt, Dh), lambda b, j: (b, 0, j, 0))
    spread = (jnp.arange(2 * H)[:, None] == jnp.arange(2 * W)[None, :] // Dh).astype(BF16)
    spread = jnp.concatenate([spread] * 3, axis=0)
    return pl.pallas_call(
        functools.partial(_dn_delta_kernel, tt=tt),
        grid=(B, S // tt),
        in_specs=[head_major, head_major, head_major, pl.BlockSpec((1, tt, 2 * H), tok),
                  pl.BlockSpec((1, tt // C, H, C), lambda b, j: (b, j, 0, 0)),
                  _const_spec((3 * 2 * H, 2 * W))],
        out_specs=pl.BlockSpec((1, tt, W), tok),
        out_shape=jax.ShapeDtypeStruct((B, S, W), F32),
        scratch_shapes=[pltpu.VMEM((H, Dh, Dh), F32)],
        compiler_params=_params(("arbitrary", "arbitrary")),
        name="dn_delta",
    )(q, k, v, gcol, grow, spread)


def _dn_out_kernel(x_ref, o_ref, gpre_ref, wz_ref, ng_ref, wout_ref, gpost_ref, y_ref, *, halves):
    H, Dh = DN_HEADS, DN_HEAD_DIM
    rows = x_ref.shape[0] // halves
    sls = [slice(i * rows, (i + 1) * rows) for i in range(halves)]
    ng = ng_ref[...]
    hs = [_rms(x_ref[sl, :], gpre_ref[...]).astype(BF16) for sl in sls]
    zs = [jnp.dot(h, wz_ref[...], preferred_element_type=F32) for h in hs]
    for i, sl in enumerate(sls):
        parts = []
        for hd in range(H):
            oh = o_ref[sl, hd * Dh:(hd + 1) * Dh]
            parts.append((_rms(oh, ng) * _silu(zs[i][:, hd * Dh:(hd + 1) * Dh])).astype(BF16))
        m = jnp.dot(jnp.concatenate(parts, axis=-1), wout_ref[...], preferred_element_type=F32)
        y_ref[sl, :] = x_ref[sl, :] + _rms(m, gpost_ref[...])


def _dn_out(x2, o2, gpre, w_in, z_block, ng, wout, gpost, tm=1024, halves=2):
    T, D = x2.shape
    W = o2.shape[1]
    row = lambda i: (i, 0)
    return pl.pallas_call(
        functools.partial(_dn_out_kernel, halves=halves),
        grid=(T // tm,),
        in_specs=[pl.BlockSpec((tm, D), row), pl.BlockSpec((tm, W), row),
                  _const_spec((1, D)), _col_block_spec(D, W, z_block), _const_spec((1, DN_HEAD_DIM)),
                  _const_spec((W, D)), _const_spec((1, D))],
        out_specs=pl.BlockSpec((tm, D), row),
        out_shape=jax.ShapeDtypeStruct((T, D), F32),
        compiler_params=_params(("parallel",)),
        name="dn_out",
    )(x2, o2, gpre, w_in, ng, wout, gpost)


def _sg_kernel(x_ref, gpre_ref, win_ref, bin_ref, lng_ref, lnb_ref, ws_ref, bst_ref, wout_ref,
               gpost_ref, y_ref, *, tm):
    G, C = SG_GROUPS, SG_CHUNK
    E = win_ref.shape[1] // 2
    Eg = E // G
    x = x_ref[...]
    h = _rms(x, gpre_ref[...]).astype(BF16)

    def in_proj(col0):
        cols = slice(col0, col0 + Eg)
        z = jnp.dot(h, win_ref[:, cols], preferred_element_type=F32) + bin_ref[:, cols]
        return 0.5 * z * (1.0 + lax.erf(z * (2.0 ** -0.5)))

    vs = [in_proj(E + g * Eg) for g in range(G)]
    us = [in_proj(g * Eg) for g in range(G)]
    tot = vs[0]
    for g in range(1, G):
        tot = tot + vs[g]
    mu = jnp.sum(tot, axis=-1, keepdims=True) * (1.0 / E)
    xcs = [v - mu for v in vs]
    tot = xcs[0] * xcs[0]
    for g in range(1, G):
        tot = tot + xcs[g] * xcs[g]
    rstd = lax.rsqrt(jnp.sum(tot, axis=-1, keepdims=True) * (1.0 / E) + LN_EPS)

    ri = lax.broadcasted_iota(jnp.int32, (C, C), 0)
    ci = lax.broadcasted_iota(jnp.int32, (C, C), 1)
    bst = bst_ref[...]
    parts = []
    for g in range(G):
        cols = slice(g * Eg, (g + 1) * Eg)
        vn = (xcs[g] * rstd * lng_ref[:, cols] + lnb_ref[:, cols]).astype(BF16)
        wc = jnp.where(ci <= ri, ws_ref[g], 0.0).astype(BF16)
        rows = []
        for c in range(tm // C):
            mixed = jnp.dot(wc, vn[c * C:(c + 1) * C], preferred_element_type=F32) + bst[:, g:g + 1]
            rows.append((us[g][c * C:(c + 1) * C] * mixed).astype(BF16))
        parts.append(jnp.concatenate(rows, axis=0))
    gated = jnp.concatenate(parts, axis=-1)
    m = jnp.dot(gated, wout_ref[...], preferred_element_type=F32)
    y_ref[...] = x + _rms(m, gpost_ref[...])


def _sg(x2, gpre, win, b_in, lng, lnb, ws, bst, wout, gpost, tm=1024):
    T, D = x2.shape
    E2 = win.shape[1]
    E = E2 // 2
    G, C = SG_GROUPS, SG_CHUNK
    row = lambda i: (i, 0)
    return pl.pallas_call(
        functools.partial(_sg_kernel, tm=tm),
        grid=(T // tm,),
        in_specs=[pl.BlockSpec((tm, D), row), _const_spec((1, D)), _const_spec((D, E2)),
                  _const_spec((1, E2)), _const_spec((1, E)), _const_spec((1, E)),
                  _const_spec((G, C, C)), _const_spec((C, G)), _const_spec((E, D)),
                  _const_spec((1, D))],
        out_specs=pl.BlockSpec((tm, D), row),
        out_shape=jax.ShapeDtypeStruct((T, D), F32),
        compiler_params=_params(("parallel",)),
        name="sg",
    )(x2, gpre, win, b_in, lng, lnb, ws, bst, wout, gpost)


def kernel(x, norm_g, ffn_w_gate, ffn_w_up, ffn_w_down, dn_w_in, dn_conv_w, dn_a_log, dn_dt_bias,
           dn_norm_g, dn_w_out, sg_w_in, sg_b_in, sg_ln_g, sg_ln_b, sg_w_s, sg_b_s, sg_w_out):
    B, S, D = x.shape
    depth = norm_g.shape[0]
    H = DN_HEADS
    W = H * DN_HEAD_DIM
    wg = ffn_w_gate.astype(BF16)
    wu = ffn_w_up.astype(BF16)
    wd = ffn_w_down.astype(BF16)
    x2 = x.reshape(B * S, D)
    for i in range(depth):
        ng = norm_g[i].reshape(6, 1, D)
        x2 = _ffn(x2, ng[0], wg, wu, wd, ng[1], i, 0)
        j = i // 2
        if i % 2 == 0:
            w_in = dn_w_in[j].astype(BF16)
            q, k, v, gcol, grow = _dn_in(x2.reshape(B, S, D), ng[2], w_in,
                                         w_in[:, 4 * W:], w_in[:, 4 * W:].T, dn_conv_w[j],
                                         dn_a_log[j], dn_dt_bias[j])
            o = _dn_delta(q, k, v, gcol, grow)
            x2 = _dn_out(x2, o.reshape(B * S, W), ng[2], w_in, 3,
                         dn_norm_g[j].reshape(1, -1), dn_w_out[j].astype(BF16), ng[3])
        else:
            E = sg_w_in.shape[2] // 2
            x2 = _sg(x2, ng[2], sg_w_in[j].astype(BF16), sg_b_in[j].reshape(1, -1),
                     sg_ln_g[j].reshape(1, E), sg_ln_b[j].reshape(1, E), sg_w_s[j], sg_b_s[j].T,
                     sg_w_out[j].astype(BF16), ng[3])
        x2 = _ffn(x2, ng[4], wg, wu, wd, ng[5], i, 1)
    return x2.reshape(B, S, D)
```

```python
import functools

import jax
import jax.numpy as jnp
from jax import lax
from jax.experimental import pallas as pl
from jax.experimental.pallas import tpu as pltpu

RMS_EPS = 1e-6
LN_EPS = 1e-5
L2_EPS = 1e-6
DN_HEADS = 8
DN_HEAD_DIM = 128
DN_CONV = 4
DN_CHUNK = 64
SUBLANES = 8
DN_GROUP_CHUNKS = 4
DN_PROJECT_AHEAD = 4
SG_GROUPS = 8
SG_CHUNK = 128

VMEM_LIMIT_BYTES = 56 * 1024 * 1024

BF16 = jnp.bfloat16
F32 = jnp.float32


def _bf16_terms(x):
    hi = x.astype(BF16)
    r1 = x - hi.astype(F32)
    mid = r1.astype(BF16)
    lo = (r1 - mid.astype(F32)).astype(BF16)
    return [hi, mid, lo]


def _rms(x, g):
    return x * lax.rsqrt(jnp.mean(x * x, axis=-1, keepdims=True) + RMS_EPS) * g


def _silu(x):
    return x * jax.nn.sigmoid(x)


def _const_spec(shape):
    nd = len(shape)
    return pl.BlockSpec(shape, lambda *_: (0,) * nd, pipeline_mode=pl.Buffered(1))


def _col_block_spec(rows, width, block):
    return pl.BlockSpec((rows, width), lambda *_: (0, block), pipeline_mode=pl.Buffered(1))


def _params(sem):
    return pltpu.CompilerParams(dimension_semantics=sem, vmem_limit_bytes=VMEM_LIMIT_BYTES)


def _ffn_kernel(x_ref, gpre_ref, wg_ref, wu_ref, wd_ref, gpost_ref, o_ref, *, parts):
    rows = x_ref.shape[0] // parts
    sls = [slice(i * rows, (i + 1) * rows) for i in range(parts)]
    hs = [_rms(x_ref[sl, :], gpre_ref[...]).astype(BF16) for sl in sls]
    ys = []
    for i, sl in enumerate(sls):
        gate = jnp.dot(hs[i], wg_ref[...], preferred_element_type=F32)
        up = jnp.dot(hs[i], wu_ref[...], preferred_element_type=F32)
        a = (_silu(gate) * up).astype(BF16)
        ys.append(jnp.dot(a, wd_ref[...], preferred_element_type=F32))
        if i > 0:
            o_ref[sls[i - 1], :] = x_ref[sls[i - 1], :] + 0.5 * _rms(ys[i - 1], gpost_ref[...])
    o_ref[sls[-1], :] = x_ref[sls[-1], :] + 0.5 * _rms(ys[-1], gpost_ref[...])


def _ffn(x2, gpre, wg, wu, wd, gpost, layer, slot, tm=1024, parts=4):
    T, D = x2.shape
    F = wg.shape[-1]
    pick = lambda r, c: pl.BlockSpec((None, None, r, c), lambda *_: (layer, slot, 0, 0),
                                     pipeline_mode=pl.Buffered(1))
    return pl.pallas_call(
        functools.partial(_ffn_kernel, parts=parts),
        grid=(T // tm,),
        in_specs=[pl.BlockSpec((tm, D), lambda i: (i, 0)),
                  _const_spec((1, D)), pick(D, F), pick(D, F), pick(F, D), _const_spec((1, D))],
        out_specs=pl.BlockSpec((tm, D), lambda i: (i, 0)),
        out_shape=jax.ShapeDtypeStruct((T, D), F32),
        compiler_params=_params(("parallel",)),
        name="ffn",
    )(x2, gpre, wg, wu, wd, gpost)


def _dn_in_kernel(x_ref, gpre_ref, wqkv_ref, wba_ref, wbat_ref, conv_ref, alog_c_ref, dtb_c_ref,
                  alog_r_ref, dtb_r_ref, perm_ref, cum_ref, cumt_ref, ones_ref,
                  q_ref, k_ref, v_ref, gcol_ref, grow_ref, *carry_refs, tm):
    H, Dh, C, P = DN_HEADS, DN_HEAD_DIM, DN_CHUNK, DN_CONV
    bw = 2 * Dh
    nb = H * Dh // bw
    Q = tm // P

    @pl.when(pl.program_id(1) == 0)
    def _():
        for carry in carry_refs:
            carry[...] = jnp.zeros(carry.shape, F32)

    x = x_ref[0]
    h_nat = _rms(x, gpre_ref[...]).astype(BF16)
    h = jnp.dot(perm_ref[...], h_nat, preferred_element_type=F32).astype(BF16)
    n_blocks = len(carry_refs)

    def project(cb):
        cols = slice(cb * bw, (cb + 1) * bw)
        return jnp.dot(h, wqkv_ref[:, cols], preferred_element_type=F32)

    ba = jnp.dot(h_nat, wba_ref[...], preferred_element_type=F32)
    bat = lax.dot_general(wbat_ref[...], h_nat, (((1,), (1,)), ((), ())),
                          preferred_element_type=F32)
    projs = {cb: project(cb) for cb in range(DN_PROJECT_AHEAD)}
    beta_c = jax.nn.sigmoid(ba[:, :H])
    g_c = -jnp.exp(alog_c_ref[...]) * jax.nn.softplus(ba[:, H:] + dtb_c_ref[...])
    g_r = -jnp.exp(alog_r_ref[...]) * jax.nn.softplus(bat[H:, :] + dtb_r_ref[...])
    gc_c = sum(jnp.dot(cum_ref[...], t, preferred_element_type=F32) for t in _bf16_terms(g_c))
    gc_r3 = jnp.dot(jnp.concatenate(_bf16_terms(g_r), axis=0), cumt_ref[...],
                    preferred_element_type=F32)
    gc_r = gc_r3[0:H] + gc_r3[H:2 * H] + gc_r3[2 * H:3 * H]
    gcol_ref[0, :, 0:H] = beta_c
    gcol_ref[0, :, H:2 * H] = gc_c
    for c in range(tm // C):
        grow_ref[0, c] = gc_r[:, c * C:(c + 1) * C]

    for cb, carry in enumerate(carry_refs):
        cols = slice(cb * bw, (cb + 1) * bw)
        proj = projs.pop(cb)
        cw = conv_ref[:, cols]
        phase = [proj[p * Q:(p + 1) * Q] for p in range(P)]
        up = {p: jnp.concatenate([carry[p - 1], phase[p]], axis=0)[7:7 + Q] for p in range(1, P)}
        for p in range(1, P):
            carry[p - 1] = phase[p][Q - 8:Q]
        ys = []
        for p in range(P):
            acc = phase[p] * cw[P - 1:P, :]
            for s in range(1, P):
                src = phase[p - s] if p >= s else up[p - s + P]
                acc = acc + src * cw[P - 1 - s:P - s, :]
            ys.append(_silu(acc))
        if cb + DN_PROJECT_AHEAD < n_blocks:
            projs[cb + DN_PROJECT_AHEAD] = project(cb + DN_PROJECT_AHEAD)
        out_ref = (q_ref, k_ref, v_ref)[cb // nb]
        y = jnp.concatenate(ys, axis=0)
        if cb < 2 * nb:
            scale = Dh ** -0.5 if cb < nb else 1.0
            ss = jnp.dot((y * y).astype(BF16), ones_ref[...], preferred_element_type=F32)
            y = y * (lax.rsqrt(ss + L2_EPS) * scale)
        for i in range(bw // Dh):
            hd = (cb % nb) * (bw // Dh) + i
            for p in range(P):
                out_ref[0, hd, pl.ds(p, Q, stride=P), :] = y[p * Q:(p + 1) * Q, i * Dh:(i + 1) * Dh]


def _dn_in(x, gpre, wqkv, wba, wbat, conv_w, alog, dtb, tm=256):
    B, S, D = x.shape
    H, Dh, C, P = DN_HEADS, DN_HEAD_DIM, DN_CHUNK, DN_CONV
    W = H * Dh
    bw = 2 * Dh
    Q = tm // P
    nat = jnp.arange(tm)
    perm = ((P * (nat % Q) + nat // Q)[:, None] == nat[None, :]).astype(BF16)
    cum = ((nat[:, None] // C == nat[None, :] // C) & (nat[None, :] <= nat[:, None])).astype(BF16)
    head_ones = (jnp.arange(bw)[:, None] // Dh == jnp.arange(bw)[None, :] // Dh).astype(BF16)
    head_major = lambda b, j: (b, 0, j, 0)
    return pl.pallas_call(
        functools.partial(_dn_in_kernel, tm=tm),
        grid=(B, S // tm),
        in_specs=[pl.BlockSpec((1, tm, D), lambda b, j: (b, j, 0)),
                  _const_spec((1, D)), _col_block_spec(D, 3 * W, 0), _const_spec((D, 2 * H)),
                  _const_spec((2 * H, D)), _const_spec((P, 3 * W)),
                  _const_spec((1, H)), _const_spec((1, H)), _const_spec((H, 1)), _const_spec((H, 1)),
                  _const_spec((tm, tm)), _const_spec((tm, tm)), _const_spec((tm, tm)),
                  _const_spec((bw, bw))],
        out_specs=[pl.BlockSpec((1, H, tm, Dh), head_major), pl.BlockSpec((1, H, tm, Dh), head_major),
                   pl.BlockSpec((1, H, tm, Dh), head_major),
                   pl.BlockSpec((1, tm, 2 * H), lambda b, j: (b, j, 0)),
                   pl.BlockSpec((1, tm // C, H, C), lambda b, j: (b, j, 0, 0))],
        out_shape=[jax.ShapeDtypeStruct((B, H, S, Dh), F32), jax.ShapeDtypeStruct((B, H, S, Dh), F32),
                   jax.ShapeDtypeStruct((B, H, S, Dh), F32), jax.ShapeDtypeStruct((B, S, 2 * H), F32),
                   jax.ShapeDtypeStruct((B, S // C, H, C), F32)],
        scratch_shapes=[pltpu.VMEM((P - 1, 8, bw), F32) for _ in range(3 * W // bw)],
        compiler_params=_params(("arbitrary", "arbitrary")),
        name="dn_in",
    )(x, gpre, wqkv, wba, wbat, conv_w, alog.reshape(1, H), dtb.reshape(1, H),
      alog.reshape(H, 1), dtb.reshape(H, 1), perm, cum, cum.T, head_ones)


def _bmm(a, b):
    return lax.dot_general(a.astype(BF16), b.astype(BF16), (((2,), (1,)), ((0,), (0,))),
                           preferred_element_type=F32)


def _bmm_nt(a, b):
    return lax.dot_general(a.astype(BF16), b.astype(BF16), (((2,), (2,)), ((0,), (0,))),
                           preferred_element_type=F32)


def _unit_lower_inverse_steps(low, ri, ci):
    C = low.shape[-1]
    eye = jnp.where(ri == ci, 1.0, 0.0).astype(F32)
    t = eye - jnp.where((ri // 2) == (ci // 2), low, 0.0)
    b = 2
    while b < C:
        off = jnp.where(((ri // (2 * b)) == (ci // (2 * b))) & ((ri // b) != (ci // b)), low, 0.0)
        if b < SUBLANES:
            t = t - _bmm(t, _bmm(off, t))
        else:
            lower = [slice(s, s + b) for s in range(b, C, 2 * b)]
            rows = lambda a: jnp.concatenate([a[:, r] for r in lower], axis=1)
            x_lo = _bmm(rows(off), t)
            zero = jnp.zeros_like(x_lo[:, :b])
            x = jnp.concatenate([p for i in range(len(lower))
                                 for p in (zero, x_lo[:, i * b:(i + 1) * b])], axis=1)
            corr = _bmm(rows(t), x)
            t = jnp.concatenate([p for i, r in enumerate(lower)
                                 for p in (t[:, r.start - b:r.start],
                                           t[:, r] - corr[:, i * b:(i + 1) * b])], axis=1)
        b *= 2
        yield None
    yield t


def _dn_delta_kernel(q_ref, k_ref, v_ref, gcol_ref, grow_ref, spread_ref, o_ref, s_ref, *, tt):
    H, Dh, C = DN_HEADS, DN_HEAD_DIM, DN_CHUNK
    nc = tt // C

    @pl.when(pl.program_id(1) == 0)
    def _():
        s_ref[...] = jnp.zeros_like(s_ref)

    ri = lax.broadcasted_iota(jnp.int32, (1, C, C), 1)
    ci = lax.broadcasted_iota(jnp.int32, (1, C, C), 2)
    causal = ci <= ri
    strict = ci < ri

    gates = gcol_ref[0]
    spread = jnp.dot(jnp.concatenate(_bf16_terms(gates), axis=1), spread_ref[...],
                     preferred_element_type=F32)

    def local_steps(chunks):
        per_head = lambda ref: jnp.stack([ref[0, hd, c * C:(c + 1) * C, :]
                                          for c in chunks for hd in range(H)])
        q, k, v = per_head(q_ref), per_head(k_ref), per_head(v_ref)
        beta = jnp.stack([spread[c * C:(c + 1) * C, hd * Dh:(hd + 1) * Dh]
                          for c in chunks for hd in range(H)])
        gc = jnp.stack([spread[c * C:(c + 1) * C, (H + hd) * Dh:(H + hd + 1) * Dh]
                        for c in chunks for hd in range(H)])
        gr = jnp.stack([grow_ref[0, c, hd:hd + 1, :] for c in chunks for hd in range(H)])
        gl = gc[:, C - 1:C, :]
        decay = jnp.where(causal, jnp.exp(jnp.where(causal, gc[:, :, :C] - gr, 0.0)), 0.0)
        egc = jnp.exp(gc)
        kb = k * beta
        kq = _bmm_nt(jnp.concatenate([kb, q], axis=1), k)
        yield None
        low = jnp.where(strict, kq[:, :C] * decay, 0.0)
        attn = kq[:, C:] * decay
        t = None
        for t in _unit_lower_inverse_steps(low, ri, ci):
            if t is None:
                yield None
        uw = _bmm(t, jnp.concatenate([v * beta, kb * egc], axis=2))
        wq = jnp.concatenate([uw[:, :, Dh:], q * egc], axis=1).astype(BF16)
        kdt = jnp.swapaxes(k * jnp.exp(gl - gc), 1, 2)
        akd = jnp.concatenate([attn, kdt], axis=1).astype(BF16)
        yield dict(u=uw[:, :, :Dh], wq=wq, akd=akd, egl=jnp.exp(gl))

    def recurrent_steps(chunks, loc, state):
        for i, c in enumerate(chunks):
            sl = slice(i * H, (i + 1) * H)
            ws = _bmm(loc["wq"][sl], state)
            v_new = loc["u"][sl] - ws[:, :C]
            yield None
            av = _bmm(loc["akd"][sl], v_new)
            o = ws[:, C:] + av[:, :C]
            for hd in range(H):
                o_ref[0, c * C:(c + 1) * C, hd * Dh:(hd + 1) * Dh] = o[hd]
            state = state * loc["egl"][sl] + av[:, C:]
            yield None
        yield state

    groups = [list(range(g, min(g + DN_GROUP_CHUNKS, nc))) for g in range(0, nc, DN_GROUP_CHUNKS)]
    state = s_ref[...]
    loc_prev = None
    for gi in range(len(groups) + 1):
        rec = recurrent_steps(groups[gi - 1], loc_prev, state) if gi > 0 else iter(())
        loc = None
        if gi < len(groups):
            for loc in local_steps(groups[gi]):
                out = next(rec, None)
                if out is not None:
                    state = out
        for out in rec:
            if out is not None:
                state = out
        loc_prev = loc
    s_ref[...] = state


def _dn_delta(q, k, v, gcol, grow, tt=1024):
    B, H, S, Dh = q.shape
    C = DN_CHUNK
    W = H * Dh
    tok = lambda b, j: (b, j, 0)
    head_major = pl.BlockSpec((1, H, tt, Dh), lambda b, j: (b, 0, j, 0))
    spread = (jnp.arange(2 * H)[:, None] == jnp.arange(2 * W)[None, :] // Dh).astype(BF16)
    spread = jnp.concatenate([spread] * 3, axis=0)
    return pl.pallas_call(
        functools.partial(_dn_delta_kernel, tt=tt),
        grid=(B, S // tt),
        in_specs=[head_major, head_major, head_major, pl.BlockSpec((1, tt, 2 * H), tok),
                  pl.BlockSpec((1, tt // C, H, C), lambda b, j: (b, j, 0, 0)),
                  _const_spec((3 * 2 * H, 2 * W))],
        out_specs=pl.BlockSpec((1, tt, W), tok),
        out_shape=jax.ShapeDtypeStruct((B, S, W), F32),
        scratch_shapes=[pltpu.VMEM((H, Dh, Dh), F32)],
        compiler_params=_params(("arbitrary", "arbitrary")),
        name="dn_delta",
    )(q, k, v, gcol, grow, spread)


def _dn_out_kernel(x_ref, o_ref, gpre_ref, wz_ref, ng_ref, wout_ref, gpost_ref, y_ref, *, halves):
    H, Dh = DN_HEADS, DN_HEAD_DIM
    rows = x_ref.shape[0] // halves
    sls = [slice(i * rows, (i + 1) * rows) for i in range(halves)]
    ng = ng_ref[...]
    hs = [_rms(x_ref[sl, :], gpre_ref[...]).astype(BF16) for sl in sls]
    zs = [jnp.dot(h, wz_ref[...], preferred_element_type=F32) for h in hs]
    for i, sl in enumerate(sls):
        parts = []
        for hd in range(H):
            oh = o_ref[sl, hd * Dh:(hd + 1) * Dh]
            parts.append((_rms(oh, ng) * _silu(zs[i][:, hd * Dh:(hd + 1) * Dh])).astype(BF16))
        m = jnp.dot(jnp.concatenate(parts, axis=-1), wout_ref[...], preferred_element_type=F32)
        y_ref[sl, :] = x_ref[sl, :] + _rms(m, gpost_ref[...])


def _dn_out(x2, o2, gpre, w_in, z_block, ng, wout, gpost, tm=1024, halves=4):
    T, D = x2.shape
    W = o2.shape[1]
    row = lambda i: (i, 0)
    return pl.pallas_call(
        functools.partial(_dn_out_kernel, halves=halves),
        grid=(T // tm,),
        in_specs=[pl.BlockSpec((tm, D), row), pl.BlockSpec((tm, W), row),
                  _const_spec((1, D)), _col_block_spec(D, W, z_block), _const_spec((1, DN_HEAD_DIM)),
                  _const_spec((W, D)), _const_spec((1, D))],
        out_specs=pl.BlockSpec((tm, D), row),
        out_shape=jax.ShapeDtypeStruct((T, D), F32),
        compiler_params=_params(("parallel",)),
        name="dn_out",
    )(x2, o2, gpre, w_in, ng, wout, gpost)


def _sg_kernel(x_ref, gpre_ref, win_ref, bin_ref, lng_ref, lnb_ref, ws_ref, bst_ref, wout_ref,
               gpost_ref, y_ref, *, tm):
    G, C = SG_GROUPS, SG_CHUNK
    E = win_ref.shape[1] // 2
    Eg = E // G
    x = x_ref[...]
    h = _rms(x, gpre_ref[...]).astype(BF16)

    def in_proj(col0):
        cols = slice(col0, col0 + Eg)
        z = jnp.dot(h, win_ref[:, cols], preferred_element_type=F32) + bin_ref[:, cols]
        return 0.5 * z * (1.0 + lax.erf(z * (2.0 ** -0.5)))

    vs = [in_proj(E + g * Eg) for g in range(G)]
    us = [in_proj(g * Eg) for g in range(G)]
    tot = vs[0]
    for g in range(1, G):
        tot = tot + vs[g]
    mu = jnp.sum(tot, axis=-1, keepdims=True) * (1.0 / E)
    xcs = [v - mu for v in vs]
    tot = xcs[0] * xcs[0]
    for g in range(1, G):
        tot = tot + xcs[g] * xcs[g]
    rstd = lax.rsqrt(jnp.sum(tot, axis=-1, keepdims=True) * (1.0 / E) + LN_EPS)

    ri = lax.broadcasted_iota(jnp.int32, (C, C), 0)
    ci = lax.broadcasted_iota(jnp.int32, (C, C), 1)
    bst = bst_ref[...]
    parts = []
    for g in range(G):
        cols = slice(g * Eg, (g + 1) * Eg)
        vn = (xcs[g] * rstd * lng_ref[:, cols] + lnb_ref[:, cols]).astype(BF16)
        wc = jnp.where(ci <= ri, ws_ref[g], 0.0).astype(BF16)
        rows = []
        for c in range(tm // C):
            mixed = jnp.dot(wc, vn[c * C:(c + 1) * C], preferred_element_type=F32) + bst[:, g:g + 1]
            rows.append((us[g][c * C:(c + 1) * C] * mixed).astype(BF16))
        parts.append(jnp.concatenate(rows, axis=0))
    gated = jnp.concatenate(parts, axis=-1)
    m = jnp.dot(gated, wout_ref[...], preferred_element_type=F32)
    y_ref[...] = x + _rms(m, gpost_ref[...])


def _sg(x2, gpre, win, b_in, lng, lnb, ws, bst, wout, gpost, tm=1024):
    T, D = x2.shape
    E2 = win.shape[1]
    E = E2 // 2
    G, C = SG_GROUPS, SG_CHUNK
    row = lambda i: (i, 0)
    return pl.pallas_call(
        functools.partial(_sg_kernel, tm=tm),
        grid=(T // tm,),
        in_specs=[pl.BlockSpec((tm, D), row), _const_spec((1, D)), _const_spec((D, E2)),
                  _const_spec((1, E2)), _const_spec((1, E)), _const_spec((1, E)),
                  _const_spec((G, C, C)), _const_spec((C, G)), _const_spec((E, D)),
                  _const_spec((1, D))],
        out_specs=pl.BlockSpec((tm, D), row),
        out_shape=jax.ShapeDtypeStruct((T, D), F32),
        compiler_params=_params(("parallel",)),
        name="sg",
    )(x2, gpre, win, b_in, lng, lnb, ws, bst, wout, gpost)


def kernel(x, norm_g, ffn_w_gate, ffn_w_up, ffn_w_down, dn_w_in, dn_conv_w, dn_a_log, dn_dt_bias,
           dn_norm_g, dn_w_out, sg_w_in, sg_b_in, sg_ln_g, sg_ln_b, sg_w_s, sg_b_s, sg_w_out):
    B, S, D = x.shape
    depth = norm_g.shape[0]
    H = DN_HEADS
    W = H * DN_HEAD_DIM
    wg = ffn_w_gate.astype(BF16)
    wu = ffn_w_up.astype(BF16)
    wd = ffn_w_down.astype(BF16)
    x2 = x.reshape(B * S, D)
    for i in range(depth):
        ng = norm_g[i].reshape(6, 1, D)
        x2 = _ffn(x2, ng[0], wg, wu, wd, ng[1], i, 0)
        j = i // 2
        if i % 2 == 0:
            w_in = dn_w_in[j].astype(BF16)
            q, k, v, gcol, grow = _dn_in(x2.reshape(B, S, D), ng[2], w_in,
                                         w_in[:, 4 * W:], w_in[:, 4 * W:].T, dn_conv_w[j],
                                         dn_a_log[j], dn_dt_bias[j])
            o = _dn_delta(q, k, v, gcol, grow)
            x2 = _dn_out(x2, o.reshape(B * S, W), ng[2], w_in, 3,
                         dn_norm_g[j].reshape(1, -1), dn_w_out[j].astype(BF16), ng[3])
        else:
            E = sg_w_in.shape[2] // 2
            x2 = _sg(x2, ng[2], sg_w_in[j].astype(BF16), sg_b_in[j].reshape(1, -1),
                     sg_ln_g[j].reshape(1, E), sg_ln_b[j].reshape(1, E), sg_w_s[j], sg_b_s[j].T,
                     sg_w_out[j].astype(BF16), ng[3])
        x2 = _ffn(x2, ng[4], wg, wu, wd, ng[5], i, 1)
    return x2.reshape(B, S, D)
```
